```python
import jax
import jax.numpy as jnp
from jax import lax
import numpy as np

D_MODEL = 1024
BATCH = 8
SEQ = 4096
DEPTH = 4

MEM_LEN = 256
D_FF = 2816
FFN_RES = 0.5
EPS = 1e-6
MIX_W = D_MODEL
CONV_W = D_MODEL // 4
CONV_LEN = 3
HG_HEAD_DIM = 64
HG_W = 3 * D_MODEL // 8
HG_HEADS = HG_W // HG_HEAD_DIM
HG_EXP_CLIP = 80.0
RW_HEAD_DIM = 64
RW_W = MIX_W - CONV_W - HG_W
RW_HEADS = RW_W // RW_HEAD_DIM
RW_DECAY_RANK = 64
RW_A_RANK = 64
RW_G_RANK = 128
RW_PROJ = 3 * RW_W + RW_DECAY_RANK + RW_A_RANK + RW_G_RANK
RW_SPLITS = (RW_W, 2 * RW_W, 3 * RW_W, 3 * RW_W + RW_DECAY_RANK, 3 * RW_W + RW_DECAY_RANK + RW_A_RANK)
RW_DECAY_SCALE = 0.606531
RW_GN_EPS = 64e-5
CONV_PROJ = 3 * CONV_W
HG_PROJ = 4 * HG_W
IN_W = CONV_PROJ + HG_PROJ + RW_PROJ
CHUNK = 64
XA_HEADS = 4
XA_HEAD_DIM = D_MODEL // XA_HEADS

kernel_name = 'hybrid_conv_hgrn2_rwkv7_macaron_memxattn'


def rmsnorm(x, g):
    xf = x.astype(jnp.float32)
    xf = xf * lax.rsqrt(jnp.mean(xf * xf, axis=-1, keepdims=True) + EPS)
    return (xf * g.astype(jnp.float32)).astype(x.dtype)


def swiglu_ffn(x, w_in, w_out):
    gate, up = jnp.split(x @ w_in, 2, axis=-1)
    return (jax.nn.silu(gate) * up) @ w_out


def short_conv_mixer(p, conv_w, conv_b):
    b_gate, c_gate, x_in = jnp.split(p, 3, axis=-1)
    z = c_gate * x_in
    zc = lax.conv_general_dilated(
        z, conv_w[:, None, :], window_strides=(1,), padding=[(CONV_LEN - 1, 0)],
        dimension_numbers=('NWC', 'WIO', 'NWC'), feature_group_count=CONV_W)
    return b_gate * (zc + conv_b)


def hgrn2_mixer(p, lb, norm_g):
    bsz, seqlen, _ = p.shape
    n_chunks = seqlen // CHUNK
    f32 = jnp.float32
    q, f_logit, i_in, g = jnp.split(p.astype(f32), 4, axis=-1)
    log_f = jax.nn.log_sigmoid(f_logit) + jnp.log1p(lb * jnp.exp(jnp.minimum(-f_logit, HG_EXP_CLIP)))
    k = (1.0 - lb) * jax.nn.sigmoid(-f_logit)
    q = jax.nn.silu(q)

    def to_chunks(t):
        return t.reshape(bsz, n_chunks, CHUNK, HG_HEADS, -1).transpose(1, 0, 3, 2, 4)

    causal = jnp.tril(jnp.ones((CHUNK, CHUNK), dtype=bool))[:, :, None]

    def chunk_step(state, inp):
        qc, kc, vc, lfc = inp
        b = jnp.cumsum(lfc, axis=2)
        o_inter = jnp.einsum('bhtd,bhde->bhte', qc * jnp.exp(b), state)
        diff = b[:, :, :, None, :] - b[:, :, None, :, :]
        dec = jnp.where(causal, jnp.exp(jnp.minimum(diff, 0.0)), 0.0)
        att = jnp.einsum('bhtd,bhsd,bhtsd->bhts', qc, kc, dec)
        o = o_inter + jnp.einsum('bhts,bhse->bhte', att, vc)
        b_last = b[:, :, -1:, :]
        state = (jnp.exp(b_last[:, :, 0, :])[..., None] * state
                 + jnp.einsum('bhsd,bhse->bhde', kc * jnp.exp(b_last - b), vc))
        return state, o

    state0 = jnp.zeros((bsz, HG_HEADS, HG_HEAD_DIM, HG_HEAD_DIM), f32)
    _, o = lax.scan(chunk_step, state0,
                    (to_chunks(q), to_chunks(k), to_chunks(i_in), to_chunks(log_f)))
    o = o.transpose(1, 0, 3, 2, 4).reshape(bsz, seqlen, HG_HEADS, HG_HEAD_DIM)
    o = o * lax.rsqrt(jnp.mean(o * o, axis=-1, keepdims=True) + EPS)
    o = o.reshape(bsz, seqlen, HG_W) * norm_g.astype(f32) * jax.nn.silu(g)
    return o.astype(p.dtype)


def rwkv7_mixer(p, mu, w0, w2, a0, a2, g2, k_k, k_a, r_k, ln_w, ln_b):
    bsz, seqlen, _ = p.shape
    f32 = jnp.float32
    cast = lambda t: t.astype(f32)
    pf = cast(p)
    p_prev = jnp.pad(pf, ((0, 0), (1, 0), (0, 0)))[:, :-1]
    pf = pf + (p_prev - pf) * cast(mu)
    r, k, v, wd, ad, gd = jnp.split(pf, RW_SPLITS, axis=-1)
    log_w = -RW_DECAY_SCALE * jax.nn.sigmoid(cast(w0) + jnp.tanh(wd) @ cast(w2))
    a = jax.nn.sigmoid(cast(a0) + ad @ cast(a2))
    g = jax.nn.sigmoid(gd) @ cast(g2)
    heads = lambda t: t.reshape(bsz, seqlen, RW_HEADS, RW_HEAD_DIM)
    kk = heads(k * cast(k_k))
    kk = kk / jnp.maximum(jnp.sqrt(jnp.sum(kk * kk, axis=-1, keepdims=True)), 1e-12)
    k = k * (1.0 + (a - 1.0) * cast(k_a))
    rh, kh, vh = heads(r), heads(k), heads(v)
    a_vec = -kk
    b_vec = kk * heads(a)
    tm = lambda t: jnp.swapaxes(t, 0, 1)

    def step(state, inp):
        r_t, w_t, k_t, v_t, a_t, b_t = inp
        sa = jnp.einsum('bhij,bhj->bhi', state, a_t)
        state = (state * w_t[:, :, None, :] + sa[..., None] * b_t[:, :, None, :]
                 + v_t[..., None] * k_t[:, :, None, :])
        return state, jnp.einsum('bhij,bhj->bhi', state, r_t)

    state0 = jnp.zeros((bsz, RW_HEADS, RW_HEAD_DIM, RW_HEAD_DIM), f32)
    _, y = lax.scan(step, state0, (tm(rh), tm(heads(jnp.exp(log_w))), tm(kh), tm(vh), tm(a_vec), tm(b_vec)))
    y = jnp.swapaxes(y, 0, 1)
    mean = jnp.mean(y, axis=-1, keepdims=True)
    var = jnp.mean(jnp.square(y - mean), axis=-1, keepdims=True)
    y = ((y - mean) * lax.rsqrt(var + RW_GN_EPS)).reshape(bsz, seqlen, RW_W) * cast(ln_w) + cast(ln_b)
    bonus = jnp.sum(rh * kh * cast(r_k), axis=-1, keepdims=True) * vh
    out = (y + bonus.reshape(bsz, seqlen, RW_W)) * g
    return out.astype(p.dtype)


def memory_cross_attention(h_n, mem_n, wq, wkv, wo):
    bsz, seqlen, _ = h_n.shape
    q = (h_n @ wq).reshape(bsz, seqlen, XA_HEADS, XA_HEAD_DIM)
    mk, mv = jnp.split(mem_n @ wkv, 2, axis=-1)
    mk = mk.reshape(bsz, -1, XA_HEADS, XA_HEAD_DIM)
    mv = mv.reshape(bsz, -1, XA_HEADS, XA_HEAD_DIM)
    s = jnp.einsum('bshd,bmhd->bhsm', q, mk).astype(jnp.float32) * (XA_HEAD_DIM ** -0.5)
    pr = jax.nn.softmax(s, axis=-1).astype(mv.dtype)
    o = jnp.einsum('bhsm,bmhd->bshd', pr, mv).reshape(bsz, seqlen, D_MODEL)
    return o @ wo


def setup_inputs(seed: int = 0) -> dict:
    key = jax.random.key(seed)
    ks = iter(jax.random.split(key, 40))
    nrm = lambda shape, scale: jax.random.normal(next(ks), shape, jnp.float32) * scale
    gain = lambda shape: 1.0 + nrm(shape, 0.02)
    L, D = DEPTH, D_MODEL
    return {
        'x': nrm((BATCH, SEQ, D), 1.0),
        'mem': nrm((BATCH, MEM_LEN, D), 1.0),
        'ffn1_norm': gain((L, D)),
        'ffn1_w_in': nrm((L, D, 2 * D_FF), D ** -0.5),
        'ffn1_w_out': nrm((L, D_FF, D), D_FF ** -0.5),
        'mix_norm': gain((L, D)),
        'w_mix_in': nrm((L, D, IN_W), D ** -0.5),
        'w_mix_out': nrm((L, MIX_W, D), MIX_W ** -0.5),
        'conv_w': nrm((L, CONV_LEN, CONV_W), CONV_LEN ** -0.5),
        'conv_b': nrm((L, CONV_W), 0.02),
        'hgrn_lb_logits': nrm((L, HG_W), 0.1),
        'hgrn_norm': gain((L, HG_W)),
        'rwkv_mu': jax.random.uniform(next(ks), (L, RW_PROJ), jnp.float32),
        'rwkv_w0': nrm((L, RW_W), 0.5),
        'rwkv_w2': nrm((L, RW_DECAY_RANK, RW_W), RW_DECAY_RANK ** -0.5),
        'rwkv_a0': nrm((L, RW_W), 0.1),
        'rwkv_a2': nrm((L, RW_A_RANK, RW_W), RW_A_RANK ** -0.5),
        'rwkv_g2': nrm((L, RW_G_RANK, RW_W), RW_G_RANK ** -0.5),
        'rwkv_k_k': 0.85 + nrm((L, RW_W), 0.05),
        'rwkv_k_a': 1.0 + nrm((L, RW_W), 0.05),
        'rwkv_r_k': nrm((L, RW_HEADS, RW_HEAD_DIM), 0.1),
        'rwkv_ln_w': gain((L, RW_W)),
        'rwkv_ln_b': nrm((L, RW_W), 0.02),
        'xattn_norm': gain((L, D)),
        'mem_norm': gain((L, D)),
        'xattn_wq': nrm((L, D, D), D ** -0.5),
        'xattn_wkv': nrm((L, D, 2 * D), D ** -0.5),
        'xattn_wo': nrm((L, D, D), D ** -0.5),
        'ffn2_norm': gain((L, D)),
        'ffn2_w_in': nrm((L, D, 2 * D_FF), D ** -0.5),
        'ffn2_w_out': nrm((L, D_FF, D), D_FF ** -0.5),
        'final_norm': gain((D,)),
    }


def reference(x, mem, ffn1_norm, ffn1_w_in, ffn1_w_out, mix_norm, w_mix_in, w_mix_out,
              conv_w, conv_b, hgrn_lb_logits, hgrn_norm, rwkv_mu, rwkv_w0, rwkv_w2, rwkv_a0,
              rwkv_a2, rwkv_g2, rwkv_k_k, rwkv_k_a, rwkv_r_k, rwkv_ln_w, rwkv_ln_b,
              xattn_norm, mem_norm, xattn_wq, xattn_wkv, xattn_wo,
              ffn2_norm, ffn2_w_in, ffn2_w_out, final_norm):
    lb_sm = jax.nn.softmax(hgrn_lb_logits.astype(jnp.float32), axis=0)
    lb_all = jnp.maximum(jnp.cumsum(lb_sm, axis=0) - lb_sm[0], 0.0)
    h = x
    for l in range(DEPTH):
        h = h + FFN_RES * swiglu_ffn(rmsnorm(h, ffn1_norm[l]), ffn1_w_in[l], ffn1_w_out[l])
        u = rmsnorm(h, mix_norm[l])
        p = u @ w_mix_in[l]
        p_conv, p_hg, p_rw = jnp.split(p, (CONV_PROJ, CONV_PROJ + HG_PROJ), axis=-1)
        y = jnp.concatenate([
            short_conv_mixer(p_conv, conv_w[l], conv_b[l]),
            hgrn2_mixer(p_hg, lb_all[l], hgrn_norm[l]),
            rwkv7_mixer(p_rw, rwkv_mu[l], rwkv_w0[l], rwkv_w2[l], rwkv_a0[l], rwkv_a2[l],
                        rwkv_g2[l], rwkv_k_k[l], rwkv_k_a[l], rwkv_r_k[l], rwkv_ln_w[l], rwkv_ln_b[l]),
        ], axis=-1)
        h = h + y @ w_mix_out[l]
        h = h + memory_cross_attention(rmsnorm(h, xattn_norm[l]), rmsnorm(mem, mem_norm[l]),
                                       xattn_wq[l], xattn_wkv[l], xattn_wo[l])
        h = h + FFN_RES * swiglu_ffn(rmsnorm(h, ffn2_norm[l]), ffn2_w_in[l], ffn2_w_out[l])
    return rmsnorm(h, final_norm)
```

```python
import functools

import numpy as np
import jax
import jax.numpy as jnp
from jax import lax
from jax.experimental import pallas as pl
from jax.experimental.pallas import tpu as pltpu

F32 = jnp.float32
BF16 = jnp.bfloat16

D_MODEL = 1024
DEPTH = 4
D_FF = 2816
FFN_RES = 0.5
EPS = 1e-6
CONV_W = 256
HD = 64
HG_W = 384
HG_HEADS = HG_W // HD
HG_EXP_CLIP = 80.0
RW_W = 384
RW_HEADS = RW_W // HD
RW_DECAY_RANK = 64
RW_A_RANK = 64
RW_G_RANK = 128
RW_PROJ = 3 * RW_W + RW_DECAY_RANK + RW_A_RANK + RW_G_RANK
RW_DECAY_SCALE = 0.606531
RW_GN_EPS = 64e-5
CONV_PROJ = 3 * CONV_W
HG_PROJ = 4 * HG_W
IN_W = CONV_PROJ + HG_PROJ + RW_PROJ
CHUNK = 64
N_LEVELS = 6
XA_HEADS = 4
XA_HEAD_DIM = D_MODEL // XA_HEADS

VMEM_LIMIT_BYTES = 56 * 1024 * 1024

NN = (((1,), (0,)), ((), ()))
NT = (((1,), (1,)), ((), ()))
TN = (((0,), (0,)), ((), ()))


def _dot(a, b, dims=NN):
    return lax.dot_general(a.astype(BF16), b.astype(BF16), dims, preferred_element_type=F32)


def _split2(x):
    hi = x.astype(BF16)
    lo = (x - hi.astype(F32)).astype(BF16)
    return hi, lo


def _split3(x):
    hi = x.astype(BF16)
    r1 = x - hi.astype(F32)
    mid = r1.astype(BF16)
    lo = (r1 - mid.astype(F32)).astype(BF16)
    return hi, mid, lo


def _dot_sel(e, x):
    hi, mid, lo = _split3(x)
    d = lambda t: lax.dot_general(e, t, NN, preferred_element_type=F32)
    return d(hi) + d(mid) + d(lo)


def _dot_sel_r(x, e):
    hi, mid, lo = _split3(x)
    d = lambda t: lax.dot_general(t, e, NN, preferred_element_type=F32)
    return d(hi) + d(mid) + d(lo)


def _dot_hp(a, b, dims=NN):
    ah, al = _split2(a)
    bh, bl = _split2(b)
    d = lambda u, w: lax.dot_general(u, w, dims, preferred_element_type=F32)
    return d(ah, bh) + (d(ah, bl) + d(al, bh))


def _rms(x, g):
    return x * lax.rsqrt(jnp.mean(x * x, axis=-1, keepdims=True) + EPS) * g


def _sigmoid(x):
    return jax.nn.sigmoid(x)


def _ffn_kernel(x_ref, g_ref, win_ref, wout_ref, *rest, n_chunks, final):
    if final:
        fg_ref, o_ref = rest
    else:
        (o_ref,) = rest
    x = x_ref[...]
    xn = _rms(x, g_ref[...]).astype(BF16)
    tf = D_FF // n_chunks
    acc = jnp.zeros_like(x)
    for c in range(n_chunks):
        gate = jnp.dot(xn, win_ref[:, c * tf:(c + 1) * tf], preferred_element_type=F32)
        up = jnp.dot(xn, win_ref[:, D_FF + c * tf:D_FF + (c + 1) * tf], preferred_element_type=F32)
        act = (gate * _sigmoid(gate) * up).astype(BF16)
        acc = acc + jnp.dot(act, wout_ref[c * tf:(c + 1) * tf, :], preferred_element_type=F32)
    out = x + FFN_RES * acc
    if final:
        out = _rms(out, fg_ref[...])
    o_ref[...] = out


def _const_spec(shape):
    nd = len(shape)
    return pl.BlockSpec(shape, lambda *_: (0,) * nd)


def _ffn(h, g, w_in, w_out, final_g=None, tm=512, n_chunks=2):
    t = h.shape[0]
    final = final_g is not None
    in_specs = [pl.BlockSpec((tm, D_MODEL), lambda i: (i, 0)),
                _const_spec((1, D_MODEL)),
                _const_spec((D_MODEL, 2 * D_FF)),
                _const_spec((D_FF, D_MODEL))]
    args = [h, g.reshape(1, D_MODEL), w_in, w_out]
    if final:
        in_specs.append(_const_spec((1, D_MODEL)))
        args.append(final_g.reshape(1, D_MODEL))
    return pl.pallas_call(
        functools.partial(_ffn_kernel, n_chunks=n_chunks, final=final),
        grid=(t // tm,),
        in_specs=in_specs,
        out_specs=pl.BlockSpec((tm, D_MODEL), lambda i: (i, 0)),
        out_shape=jax.ShapeDtypeStruct((t, D_MODEL), F32),
        compiler_params=pltpu.CompilerParams(dimension_semantics=("arbitrary",),
                                             vmem_limit_bytes=VMEM_LIMIT_BYTES),
        name="ffn_final" if final else "ffn",
    )(*args)


def _mixin_kernel(x_ref, g_ref, w_ref, pc_ref, ph_ref, pr_ref):
    xn = _rms(x_ref[...], g_ref[...]).astype(BF16)
    p = jnp.dot(xn, w_ref[...], preferred_element_type=F32)
    pc_ref[...] = p[:, :CONV_PROJ]
    ph_ref[...] = p[:, CONV_PROJ:CONV_PROJ + HG_PROJ]
    pr_ref[...] = p[:, CONV_PROJ + HG_PROJ:]


def _mix_in(h, g, w, tm=512):
    t = h.shape[0]
    row = lambda n: pl.BlockSpec((tm, n), lambda i: (i, 0))
    return pl.pallas_call(
        _mixin_kernel,
        grid=(t // tm,),
        in_specs=[row(D_MODEL), _const_spec((1, D_MODEL)), _const_spec((D_MODEL, IN_W))],
        out_specs=[row(CONV_PROJ), row(HG_PROJ), row(RW_PROJ)],
        out_shape=[jax.ShapeDtypeStruct((t, CONV_PROJ), F32),
                   jax.ShapeDtypeStruct((t, HG_PROJ), F32),
                   jax.ShapeDtypeStruct((t, RW_PROJ), F32)],
        compiler_params=pltpu.CompilerParams(dimension_semantics=("arbitrary",),
                                             vmem_limit_bytes=VMEM_LIMIT_BYTES),
        name="mix_in",
    )(h, g.reshape(1, D_MODEL), w)


def _conv_kernel(p_ref, w_ref, b_ref, y_ref, carry_ref):
    @pl.when(pl.program_id(1) == 0)
    def _():
        carry_ref[...] = jnp.zeros_like(carry_ref)

    p = p_ref[...]
    tb = p.shape[0]
    b_gate = p[:, :CONV_W]
    z = p[:, CONV_W:2 * CONV_W] * p[:, 2 * CONV_W:]
    row = lax.broadcasted_iota(jnp.int32, z.shape, 0)
    c0 = carry_ref[0:1, :]
    c1 = carry_ref[1:2, :]
    z1 = jnp.where(row == 0, c1, pltpu.roll(z, 1, 0))
    z2 = jnp.where(row == 0, c0, jnp.where(row == 1, c1, pltpu.roll(z, 2, 0)))
    w = w_ref[...]
    zc = w[0:1, :] * z2 + w[1:2, :] * z1 + w[2:3, :] * z
    y_ref[...] = (b_gate * (zc + b_ref[...])).astype(y_ref.dtype)
    carry_ref[0:2, :] = z[tb - 2:tb, :]


def _conv_mixer(p, w, b, bsz, seqlen, tb=512):
    nb = seqlen // tb
    return pl.pallas_call(
        _conv_kernel,
        grid=(bsz, nb),
        in_specs=[pl.BlockSpec((tb, CONV_PROJ), lambda bi, j: (bi * nb + j, 0)),
                  _const_spec((3, CONV_W)), _const_spec((1, CONV_W))],
        out_specs=pl.BlockSpec((tb, CONV_W), lambda bi, j: (bi * nb + j, 0)),
        out_shape=jax.ShapeDtypeStruct((bsz * seqlen, CONV_W), BF16),
        scratch_shapes=[pltpu.VMEM((8, CONV_W), F32)],
        compiler_params=pltpu.CompilerParams(dimension_semantics=("arbitrary", "arbitrary")),
        name="conv_mixer",
    )(p, w, b.reshape(1, CONV_W))


def _hgrn_select_matrix():
    c = CHUNK
    e = np.zeros((2 + N_LEVELS, c, c), np.float32)
    for i in range(c):
        e[0, i, :i + 1] = 1.0
        e[1, i, i + 1:] = 1.0
        for l in range(N_LEVELS):
            half = 1 << l
            m = (i // (2 * half)) * 2 * half + half - 1
            if i > m:
                e[2 + l, i, m + 1:i + 1] = 1.0
            else:
                e[2 + l, i, i + 1:m + 1] = 1.0
    return e.reshape((2 + N_LEVELS) * c, c)


def _head_group_matrix(width):
    idx = np.arange(width) // HD
    return (idx[:, None] == idx[None, :]).astype(np.float32)


def _tri_incl_matrix():
    return np.tril(np.ones((CHUNK, CHUNK), np.float32))


def _hgrn_kernel(p_ref, lbl_ref, ng_ref, e_ref, gm_ref, y_ref, st_ref, *, layer):
    @pl.when(pl.program_id(1) == 0)
    def _():
        st_ref[...] = jnp.zeros_like(st_ref)

    p = p_ref[...]
    q = p[:, :HG_W]
    z = p[:, HG_W:2 * HG_W]
    v = p[:, 2 * HG_W:3 * HG_W]
    gt = p[:, 3 * HG_W:]

    lg = lbl_ref[...]
    ex = jnp.exp(lg - jnp.max(lg, axis=0, keepdims=True))
    sm = ex / jnp.sum(ex, axis=0, keepdims=True)
    lb = jnp.maximum(jnp.sum(sm[:layer + 1], axis=0, keepdims=True) - sm[0:1], 0.0)

    log_sig = jnp.minimum(z, 0.0) - jnp.log1p(jnp.exp(-jnp.abs(z)))
    lf = log_sig + jnp.log1p(lb * jnp.exp(jnp.minimum(-z, HG_EXP_CLIP)))
    k = (1.0 - lb) * _sigmoid(-z)
    qs = q * _sigmoid(q)

    sums = jnp.minimum(_dot_sel(e_ref[...], lf), 0.0)
    b = sums[0:CHUNK]
    qb = qs * jnp.exp(b)
    kd = k * jnp.exp(sums[CHUNK:2 * CHUNK])
    e_last = jnp.exp(b[CHUNK - 1:CHUNK])

    rowi = lax.broadcasted_iota(jnp.int32, (CHUNK, 1), 0)
    ql, kl = [], []
    for l in range(N_LEVELS):
        w = jnp.exp(sums[(2 + l) * CHUNK:(3 + l) * CHUNK])
        right = ((rowi >> l) & 1) == 1
        ql.append(jnp.where(right, qs * w, 0.0).astype(BF16))
        kl.append(jnp.where(right, 0.0, k * w).astype(BF16))

    ti = lax.broadcasted_iota(jnp.int32, (CHUNK, CHUNK), 0)
    si = lax.broadcasted_iota(jnp.int32, (CHUNK, CHUNK), 1)
    xr = ti ^ si
    lower = ti > si
    qs_b, k_b, v_b, qb_b, kd_b = (t.astype(BF16) for t in (qs, k, v, qb, kd))

    outs = []
    for h in range(HG_HEADS):
        sl = slice(h * HD, (h + 1) * HD)
        att = jnp.where(ti == si, _dot(qs_b[:, sl], k_b[:, sl], NT), 0.0)
        for l in range(N_LEVELS):
            pick = lower & (xr >= (1 << l)) & (xr < (2 << l))
            att = jnp.where(pick, _dot(ql[l][:, sl], kl[l][:, sl], NT), att)
        st = st_ref[h]
        o = _dot(qb_b[:, sl], st, NT) + _dot(att, v_b[:, sl])
        st_ref[h] = st * e_last[:, sl] + _dot(v_b[:, sl], kd_b[:, sl], TN)
        outs.append(o)
    o = jnp.concatenate(outs, axis=-1)
    ms = _dot_sel_r(o * o, gm_ref[...]) * (1.0 / HD)
    o = o * lax.rsqrt(ms + EPS)
    y_ref[...] = (o * ng_ref[...] * (gt * _sigmoid(gt))).astype(y_ref.dtype)


def _hgrn_mixer(p, lb_logits, norm_g, layer, bsz, seqlen):
    nc = seqlen // CHUNK
    e = jnp.asarray(_hgrn_select_matrix(), BF16)
    gm = jnp.asarray(_head_group_matrix(HG_W), BF16)
    return pl.pallas_call(
        functools.partial(_hgrn_kernel, layer=layer),
        grid=(bsz, nc),
        in_specs=[pl.BlockSpec((CHUNK, HG_PROJ), lambda bi, j: (bi * nc + j, 0)),
                  _const_spec((DEPTH, HG_W)), _const_spec((1, HG_W)),
                  _const_spec(e.shape), _const_spec(gm.shape)],
        out_specs=pl.BlockSpec((CHUNK, HG_W), lambda bi, j: (bi * nc + j, 0)),
        out_shape=jax.ShapeDtypeStruct((bsz * seqlen, HG_W), BF16),
        scratch_shapes=[pltpu.VMEM((HG_HEADS, HD, HD), F32)],
        compiler_params=pltpu.CompilerParams(dimension_semantics=("arbitrary", "arbitrary")),
        name="hgrn2_mixer",
    )(p, lb_logits, norm_g.reshape(1, HG_W), e, gm)


def _rwkv_kernel(p_ref, mu_ref, w0_ref, w2_ref, a0_ref, a2_ref, g2_ref, kk_ref, ka_ref, rk_ref,
                 lnw_ref, lnb_ref, tri_ref, gm_ref, y_ref, st_ref, prev_ref):
    @pl.when(pl.program_id(1) == 0)
    def _():
        st_ref[...] = jnp.zeros_like(st_ref)
        prev_ref[...] = jnp.zeros_like(prev_ref)

    p = p_ref[...]
    rowp = lax.broadcasted_iota(jnp.int32, p.shape, 0)
    p_prev = jnp.where(rowp == 0, prev_ref[0:1, :], pltpu.roll(p, 1, 0))
    prev_ref[0:1, :] = p[CHUNK - 1:CHUNK, :]
    pf = p + (p_prev - p) * mu_ref[...]

    r = pf[:, :RW_W]
    k = pf[:, RW_W:2 * RW_W]
    v = pf[:, 2 * RW_W:3 * RW_W]
    o1 = 3 * RW_W
    wd = pf[:, o1:o1 + RW_DECAY_RANK]
    ad = pf[:, o1 + RW_DECAY_RANK:o1 + RW_DECAY_RANK + RW_A_RANK]
    gd = pf[:, o1 + RW_DECAY_RANK + RW_A_RANK:]

    lw = -RW_DECAY_SCALE * _sigmoid(w0_ref[...] + _dot(jnp.tanh(wd), w2_ref[...]))
    a = _sigmoid(a0_ref[...] + _dot(ad, a2_ref[...]))
    g = _dot(_sigmoid(gd), g2_ref[...])

    gm = gm_ref[...]
    kk = k * kk_ref[...]
    kk = kk / jnp.maximum(jnp.sqrt(_dot_sel_r(kk * kk, gm)), 1e-12)
    k = k * (1.0 + (a - 1.0) * ka_ref[...])
    a_vec = -kk
    b_vec = kk * a

    cw = _dot_sel(tri_ref[...], lw)
    cw_last = cw[CHUNK - 1:CHUNK]
    e_in = jnp.exp(cw)
    e_out = jnp.exp(-cw)
    e_rest = jnp.exp(cw_last - cw)
    a_t = a_vec * jnp.exp(cw - lw)
    r_t = r * e_in
    b_c = b_vec * e_out
    k_c = k * e_out
    b_h = b_vec * e_rest
    k_h = k * e_rest
    w_c = jnp.exp(cw_last)

    ti = lax.broadcasted_iota(jnp.int32, (CHUNK, CHUNK), 0)
    si = lax.broadcasted_iota(jnp.int32, (CHUNK, CHUNK), 1)
    strict = ti > si
    incl = ti >= si
    eye = (ti == si).astype(F32)

    outs = []
    for h in range(RW_HEADS):
        sl = slice(h * HD, (h + 1) * HD)
        at, rt, bc, kc, vh = a_t[:, sl], r_t[:, sl], b_c[:, sl], k_c[:, sl], v[:, sl]
        s0 = st_ref[h]
        n_ab = jnp.where(strict, _dot_hp(at, bc, NT), 0.0)
        a_ak = jnp.where(strict, _dot_hp(at, kc, NT), 0.0)
        a_rb = jnp.where(incl, _dot_hp(rt, bc, NT), 0.0)
        a_rk = jnp.where(incl, _dot_hp(rt, kc, NT), 0.0)
        inv = eye + n_ab
        npow = n_ab
        for _ in range(N_LEVELS - 1):
            npow = _dot_hp(npow, npow)
            inv = inv + _dot_hp(npow, inv)
        x = _dot_hp(at, s0, NT) + _dot_hp(a_ak, vh)
        u = _dot_hp(inv, x)
        yh = _dot_hp(rt, s0, NT) + _dot_hp(a_rb, u) + _dot_hp(a_rk, vh)
        st_ref[h] = s0 * w_c[:, sl] + _dot_hp(u, b_h[:, sl], TN) + _dot_hp(vh, k_h[:, sl], TN)
        outs.append(yh)
    y = jnp.concatenate(outs, axis=-1)

    inv_hd = 1.0 / HD
    mean = _dot_sel_r(y, gm) * inv_hd
    yc = y - mean
    var = _dot_sel_r(yc * yc, gm) * inv_hd
    yn = yc * lax.rsqrt(var + RW_GN_EPS) * lnw_ref[...] + lnb_ref[...]
    bonus = _dot_sel_r(r * k * rk_ref[...], gm) * v
    y_ref[...] = ((yn + bonus) * g).astype(y_ref.dtype)


def _rwkv_mixer(p, mu, w0, w2, a0, a2, g2, k_k, k_a, r_k, ln_w, ln_b, bsz, seqlen):
    nc = seqlen // CHUNK
    tri = jnp.asarray(_tri_incl_matrix(), BF16)
    gm = jnp.asarray(_head_group_matrix(RW_W), BF16)
    vec = lambda t: t.reshape(1, -1)
    args = [p, vec(mu), vec(w0), w2.astype(BF16), vec(a0), a2.astype(BF16), g2.astype(BF16),
            vec(k_k), vec(k_a), vec(r_k), vec(ln_w), vec(ln_b), tri, gm]
    in_specs = [pl.BlockSpec((CHUNK, RW_PROJ), lambda bi, j: (bi * nc + j, 0))]
    in_specs += [_const_spec(t.shape) for t in args[1:]]
    return pl.pallas_call(
        _rwkv_kernel,
        grid=(bsz, nc),
        in_specs=in_specs,
        out_specs=pl.BlockSpec((CHUNK, RW_W), lambda bi, j: (bi * nc + j, 0)),
        out_shape=jax.ShapeDtypeStruct((bsz * seqlen, RW_W), BF16),
        scratch_shapes=[pltpu.VMEM((RW_HEADS, HD, HD), F32), pltpu.VMEM((8, RW_PROJ), F32)],
        compiler_params=pltpu.CompilerParams(dimension_semantics=("arbitrary", "arbitrary")),
        name="rwkv7_mixer",
    )(*args)


def _memkv_kernel(m_ref, g_ref, w_ref, k_ref, v_ref):
    mn = _rms(m_ref[...], g_ref[...]).astype(BF16)
    kv = jnp.dot(mn, w_ref[...], preferred_element_type=F32)
    k_ref[...] = kv[:, :D_MODEL].astype(BF16)
    v_ref[...] = kv[:, D_MODEL:].astype(BF16)


def _mem_kv(mem2d, g, wkv, mem_len):
    t = mem2d.shape[0]
    row = pl.BlockSpec((mem_len, D_MODEL), lambda i: (i, 0))
    return pl.pallas_call(
        _memkv_kernel,
        grid=(t // mem_len,),
        in_specs=[row, _const_spec((1, D_MODEL)), _const_spec((D_MODEL, 2 * D_MODEL))],
        out_specs=[row, row],
        out_shape=[jax.ShapeDtypeStruct((t, D_MODEL), BF16)] * 2,
        compiler_params=pltpu.CompilerParams(dimension_semantics=("arbitrary",),
                                             vmem_limit_bytes=VMEM_LIMIT_BYTES),
        name="mem_kv",
    )(mem2d, g.reshape(1, D_MODEL), wkv)


def _xattn_kernel(h_ref, yc_ref, yh_ref, yr_ref, wmo_ref, g_ref, wq_ref, mk_ref, mv_ref, wo_ref, o_ref):
    h = h_ref[...]
    h = h + jnp.dot(yc_ref[...], wmo_ref[0:CONV_W, :], preferred_element_type=F32)
    h = h + jnp.dot(yh_ref[...], wmo_ref[CONV_W:CONV_W + HG_W, :], preferred_element_type=F32)
    h = h + jnp.dot(yr_ref[...], wmo_ref[CONV_W + HG_W:, :], preferred_element_type=F32)
    hn = _rms(h, g_ref[...]).astype(BF16)
    q = jnp.dot(hn, wq_ref[...], preferred_element_type=F32).astype(BF16)
    scale = XA_HEAD_DIM ** -0.5
    outs = []
    for hd in range(XA_HEADS):
        sl = slice(hd * XA_HEAD_DIM, (hd + 1) * XA_HEAD_DIM)
        s = lax.dot_general(q[:, sl], mk_ref[:, sl], NT, preferred_element_type=F32) * scale
        pe = jnp.exp(s - jnp.max(s, axis=-1, keepdims=True))
        pr = pe / jnp.sum(pe, axis=-1, keepdims=True)
        outs.append(jnp.dot(pr.astype(BF16), mv_ref[:, sl], preferred_element_type=F32).astype(BF16))
    o = jnp.concatenate(outs, axis=-1)
    o_ref[...] = h + jnp.dot(o, wo_ref[...], preferred_element_type=F32)


def _mixout_xattn(h, yc, yh, yr, w_mix_out, g, wq, mk, mv, wo, bsz, seqlen, mem_len, tq=512):
    nq = seqlen // tq
    row = lambda n: pl.BlockSpec((tq, n), lambda bi, j: (bi * nq + j, 0))
    memspec = pl.BlockSpec((mem_len, D_MODEL), lambda bi, j: (bi, 0))
    return pl.pallas_call(
        _xattn_kernel,
        grid=(bsz, nq),
        in_specs=[row(D_MODEL), row(CONV_W), row(HG_W), row(RW_W),
                  _const_spec((D_MODEL, D_MODEL)), _const_spec((1, D_MODEL)),
                  _const_spec((D_MODEL, D_MODEL)), memspec, memspec,
                  _const_spec((D_MODEL, D_MODEL))],
        out_specs=row(D_MODEL),
        out_shape=jax.ShapeDtypeStruct((bsz * seqlen, D_MODEL), F32),
        compiler_params=pltpu.CompilerParams(dimension_semantics=("arbitrary", "arbitrary"),
                                             vmem_limit_bytes=VMEM_LIMIT_BYTES),
        name="mixout_xattn",
    )(h, yc, yh, yr, w_mix_out, g.reshape(1, D_MODEL), wq, mk, mv, wo)


def kernel(x, mem, ffn1_norm, ffn1_w_in, ffn1_w_out, mix_norm, w_mix_in, w_mix_out, conv_w, conv_b, hgrn_lb_logits, hgrn_norm, rwkv_mu, rwkv_w0, rwkv_w2, rwkv_a0, rwkv_a2, rwkv_g2, rwkv_k_k, rwkv_k_a, rwkv_r_k, rwkv_ln_w, rwkv_ln_b, xattn_norm, mem_norm, xattn_wq, xattn_wkv, xattn_wo, ffn2_norm, ffn2_w_in, ffn2_w_out, final_norm):
    bsz, seqlen, _ = x.shape
    mem_len = mem.shape[1]
    depth = ffn1_norm.shape[0]
    h = x.reshape(bsz * seqlen, D_MODEL)
    mem2d = mem.reshape(bsz * mem_len, D_MODEL)
    bf = lambda w: w.astype(BF16)
    for l in range(depth):
        h = _ffn(h, ffn1_norm[l], bf(ffn1_w_in[l]), bf(ffn1_w_out[l]))
        p_conv, p_hg, p_rw = _mix_in(h, mix_norm[l], bf(w_mix_in[l]))
        y_conv = _conv_mixer(p_conv, conv_w[l], conv_b[l], bsz, seqlen)
        y_hg = _hgrn_mixer(p_hg, hgrn_lb_logits, hgrn_norm[l], l, bsz, seqlen)
        y_rw = _rwkv_mixer(p_rw, rwkv_mu[l], rwkv_w0[l], rwkv_w2[l], rwkv_a0[l], rwkv_a2[l],
                           rwkv_g2[l], rwkv_k_k[l], rwkv_k_a[l], rwkv_r_k[l], rwkv_ln_w[l],
                           rwkv_ln_b[l], bsz, seqlen)
        mk, mv = _mem_kv(mem2d, mem_norm[l], bf(xattn_wkv[l]), mem_len)
        h = _mixout_xattn(h, y_conv, y_hg, y_rw, bf(w_mix_out[l]), xattn_norm[l], bf(xattn_wq[l]),
                          mk, mv, bf(xattn_wo[l]), bsz, seqlen, mem_len)
        h = _ffn(h, ffn2_norm[l], bf(ffn2_w_in[l]), bf(ffn2_w_out[l]),
                 final_g=final_norm if l == depth - 1 else None)
    return h.reshape(bsz, seqlen, D_MODEL)
```

```python
import functools

import numpy as np
import jax
import jax.numpy as jnp
from jax import lax
from jax.experimental import pallas as pl
from jax.experimental.pallas import tpu as pltpu

F32 = jnp.float32
BF16 = jnp.bfloat16

D_MODEL = 1024
DEPTH = 4
D_FF = 2816
FFN_RES = 0.5
EPS = 1e-6
CONV_W = 256
HD = 64
PW = 2 * HD
HG_W = 384
HG_PAIRS = HG_W // PW
HG_EXP_CLIP = 80.0
RW_W = 384
RW_PAIRS = RW_W // PW
RW_DECAY_RANK = 64
RW_A_RANK = 64
RW_G_RANK = 128
RW_PROJ = 3 * RW_W + RW_DECAY_RANK + RW_A_RANK + RW_G_RANK
RW_DECAY_SCALE = 0.606531
RW_GN_EPS = 64e-5
CONV_PROJ = 3 * CONV_W
HG_PROJ = 4 * HG_W
IN_W = CONV_PROJ + HG_PROJ + RW_PROJ
CHUNK = 64
N_LEVELS = 6
MIX_BLOCK_CHUNKS = 4
XA_HEADS = 4
XA_HEAD_DIM = D_MODEL // XA_HEADS

VMEM_LIMIT_BYTES = 56 * 1024 * 1024

NN = (((1,), (0,)), ((), ()))
NT = (((1,), (1,)), ((), ()))
TN = (((0,), (0,)), ((), ()))


def _dot(a, b, dims=NN):
    return lax.dot_general(a.astype(BF16), b.astype(BF16), dims, preferred_element_type=F32)


def _split3(x):
    hi = x.astype(BF16)
    r1 = x - hi.astype(F32)
    mid = r1.astype(BF16)
    lo = (r1 - mid.astype(F32)).astype(BF16)
    return hi, mid, lo


def _dot_sel(e, x):
    hi, mid, lo = _split3(x)
    d = lambda t: lax.dot_general(e, t, NN, preferred_element_type=F32)
    return d(hi) + d(mid) + d(lo)


def _dot_sel_r(x, e):
    hi, mid, lo = _split3(x)
    d = lambda t: lax.dot_general(t, e, NN, preferred_element_type=F32)
    return d(hi) + d(mid) + d(lo)


def _rms(x, g):
    return x * lax.rsqrt(jnp.mean(x * x, axis=-1, keepdims=True) + EPS) * g


def _sigmoid(x):
    return jax.nn.sigmoid(x)


def _ffn_kernel(x_ref, g_ref, win_ref, wout_ref, *rest, n_chunks, final):
    if final:
        fg_ref, o_ref = rest
    else:
        (o_ref,) = rest
    x = x_ref[...]
    xn = _rms(x, g_ref[...]).astype(BF16)
    tf = D_FF // n_chunks
    acc = jnp.zeros_like(x)
    for c in range(n_chunks):
        gate = jnp.dot(xn, win_ref[:, c * tf:(c + 1) * tf], preferred_element_type=F32)
        up = jnp.dot(xn, win_ref[:, D_FF + c * tf:D_FF + (c + 1) * tf], preferred_element_type=F32)
        act = (gate * _sigmoid(gate) * up).astype(BF16)
        acc = acc + jnp.dot(act, wout_ref[c * tf:(c + 1) * tf, :], preferred_element_type=F32)
    out = x + FFN_RES * acc
    if final:
        out = _rms(out, fg_ref[...])
    o_ref[...] = out


def _const_spec(shape):
    nd = len(shape)
    return pl.BlockSpec(shape, lambda *_: (0,) * nd)


def _ffn(h, g, w_in, w_out, final_g=None, tm=512, n_chunks=2):
    t = h.shape[0]
    final = final_g is not None
    in_specs = [pl.BlockSpec((tm, D_MODEL), lambda i: (i, 0)),
                _const_spec((1, D_MODEL)),
                _const_spec((D_MODEL, 2 * D_FF)),
                _const_spec((D_FF, D_MODEL))]
    args = [h, g.reshape(1, D_MODEL), w_in, w_out]
    if final:
        in_specs.append(_const_spec((1, D_MODEL)))
        args.append(final_g.reshape(1, D_MODEL))
    return pl.pallas_call(
        functools.partial(_ffn_kernel, n_chunks=n_chunks, final=final),
        grid=(t // tm,),
        in_specs=in_specs,
        out_specs=pl.BlockSpec((tm, D_MODEL), lambda i: (i, 0)),
        out_shape=jax.ShapeDtypeStruct((t, D_MODEL), F32),
        compiler_params=pltpu.CompilerParams(dimension_semantics=("arbitrary",),
                                             vmem_limit_bytes=VMEM_LIMIT_BYTES),
        name="ffn_final" if final else "ffn",
    )(*args)


def _mixin_kernel(x_ref, g_ref, w_ref, pc_ref, ph_ref, pr_ref):
    xn = _rms(x_ref[...], g_ref[...]).astype(BF16)
    p = jnp.dot(xn, w_ref[...], preferred_element_type=F32)
    pc_ref[...] = p[:, :CONV_PROJ]
    ph_ref[...] = p[:, CONV_PROJ:CONV_PROJ + HG_PROJ]
    pr_ref[...] = p[:, CONV_PROJ + HG_PROJ:]


def _mix_in(h, g, w, tm=512):
    t = h.shape[0]
    row = lambda n: pl.BlockSpec((tm, n), lambda i: (i, 0))
    return pl.pallas_call(
        _mixin_kernel,
        grid=(t // tm,),
        in_specs=[row(D_MODEL), _const_spec((1, D_MODEL)), _const_spec((D_MODEL, IN_W))],
        out_specs=[row(CONV_PROJ), row(HG_PROJ), row(RW_PROJ)],
        out_shape=[jax.ShapeDtypeStruct((t, CONV_PROJ), F32),
                   jax.ShapeDtypeStruct((t, HG_PROJ), F32),
                   jax.ShapeDtypeStruct((t, RW_PROJ), F32)],
        compiler_params=pltpu.CompilerParams(dimension_semantics=("arbitrary",),
                                             vmem_limit_bytes=VMEM_LIMIT_BYTES),
        name="mix_in",
    )(h, g.reshape(1, D_MODEL), w)


def _conv_kernel(p_ref, w_ref, b_ref, y_ref, carry_ref):
    @pl.when(pl.program_id(1) == 0)
    def _():
        carry_ref[...] = jnp.zeros_like(carry_ref)

    p = p_ref[...]
    tb = p.shape[0]
    b_gate = p[:, :CONV_W]
    z = p[:, CONV_W:2 * CONV_W] * p[:, 2 * CONV_W:]
    row = lax.broadcasted_iota(jnp.int32, z.shape, 0)
    c0 = carry_ref[0:1, :]
    c1 = carry_ref[1:2, :]
    z1 = jnp.where(row == 0, c1, pltpu.roll(z, 1, 0))
    z2 = jnp.where(row == 0, c0, jnp.where(row == 1, c1, pltpu.roll(z, 2, 0)))
    w = w_ref[...]
    zc = w[0:1, :] * z2 + w[1:2, :] * z1 + w[2:3, :] * z
    y_ref[...] = (b_gate * (zc + b_ref[...])).astype(y_ref.dtype)
    carry_ref[0:2, :] = z[tb - 2:tb, :]


def _conv_mixer(p, w, b, bsz, seqlen, tb=512):
    nb = seqlen // tb
    return pl.pallas_call(
        _conv_kernel,
        grid=(bsz, nb),
        in_specs=[pl.BlockSpec((tb, CONV_PROJ), lambda bi, j: (bi * nb + j, 0)),
                  _const_spec((3, CONV_W)), _const_spec((1, CONV_W))],
        out_specs=pl.BlockSpec((tb, CONV_W), lambda bi, j: (bi * nb + j, 0)),
        out_shape=jax.ShapeDtypeStruct((bsz * seqlen, CONV_W), BF16),
        scratch_shapes=[pltpu.VMEM((8, CONV_W), F32)],
        compiler_params=pltpu.CompilerParams(dimension_semantics=("arbitrary", "arbitrary")),
        name="conv_mixer",
    )(p, w, b.reshape(1, CONV_W))


def _hgrn_select_matrix():
    c = CHUNK
    e = np.zeros((2 + N_LEVELS, c, c), np.float32)
    for i in range(c):
        e[0, i, :i + 1] = 1.0
        e[1, i, i + 1:] = 1.0
        for l in range(N_LEVELS):
            half = 1 << l
            m = (i // (2 * half)) * 2 * half + half - 1
            if i > m:
                e[2 + l, i, m + 1:i + 1] = 1.0
            else:
                e[2 + l, i, i + 1:m + 1] = 1.0
    return e.reshape((2 + N_LEVELS) * c, c)


def _head_group_matrix(width):
    idx = np.arange(width) // HD
    return (idx[:, None] == idx[None, :]).astype(np.float32)


def _block_select_matrix(n_chunks):
    i = np.arange(n_chunks * CHUNK)
    same = (i[:, None] // CHUNK) == (i[None, :] // CHUNK)
    incl = same & (i[None, :] <= i[:, None])
    suff = same & (i[None, :] > i[:, None])
    return np.concatenate([incl, suff], 0).astype(np.float32)


def _pair_masks():
    lane_a = lax.broadcasted_iota(jnp.int32, (CHUNK, PW), 1) < HD
    bi = lax.broadcasted_iota(jnp.int32, (PW, PW), 0) < HD
    bj = lax.broadcasted_iota(jnp.int32, (PW, PW), 1) < HD
    return lane_a, bi == bj


def _hgrn_kernel(p_ref, lbl_ref, ng_ref, e_ref, gm_ref, y_ref, st_ref, *, layer, n_chunks):
    @pl.when(pl.program_id(1) == 0)
    def _():
        st_ref[...] = jnp.zeros_like(st_ref)

    p = p_ref[...]
    q = p[:, :HG_W]
    z = p[:, HG_W:2 * HG_W]
    v = p[:, 2 * HG_W:3 * HG_W]
    gt = p[:, 3 * HG_W:]

    lg = lbl_ref[...]
    ex = jnp.exp(lg - jnp.max(lg, axis=0, keepdims=True))
    sm = ex / jnp.sum(ex, axis=0, keepdims=True)
    lb = jnp.maximum(jnp.sum(sm[:layer + 1], axis=0, keepdims=True) - sm[0:1], 0.0)

    log_sig = jnp.minimum(z, 0.0) - jnp.log1p(jnp.exp(-jnp.abs(z)))
    lf = log_sig + jnp.log1p(lb * jnp.exp(jnp.minimum(-z, HG_EXP_CLIP)))
    k = (1.0 - lb) * _sigmoid(-z)
    qs = q * _sigmoid(q)

    lane_a, blockdiag = _pair_masks()
    rowi = lax.broadcasted_iota(jnp.int32, (CHUNK, 1), 0)
    ti = lax.broadcasted_iota(jnp.int32, (PW, CHUNK), 0) & (CHUNK - 1)
    si = lax.broadcasted_iota(jnp.int32, (PW, CHUNK), 1)
    xr = ti ^ si
    lower = ti > si
    diag = ti == si
    picks = [lower & (xr >= (1 << l)) & (xr < (2 << l)) for l in range(N_LEVELS)]
    stack = lambda t: jnp.concatenate([jnp.where(lane_a, t, 0.0), jnp.where(lane_a, 0.0, t)], axis=0)

    e_mat = e_ref[...]
    inst = [(c, pr) for c in range(n_chunks) for pr in range(HG_PAIRS)]
    rows = lambda c: slice(c * CHUNK, (c + 1) * CHUNK)
    lanes = lambda pr: slice(pr * PW, (pr + 1) * PW)
    sums = [jnp.minimum(_dot_sel(e_mat, lf[rows(c)]), 0.0) for c in range(n_chunks)]
    qb, kd, e_last, ql, kl = [], [], [], [], []
    for c in range(n_chunks):
        s_c, qs_c, k_c = sums[c], qs[rows(c)], k[rows(c)]
        qb.append(qs_c * jnp.exp(s_c[0:CHUNK]))
        kd.append(k_c * jnp.exp(s_c[CHUNK:2 * CHUNK]))
        e_last.append(jnp.exp(s_c[CHUNK - 1:CHUNK]))
        ql_c, kl_c = [], []
        for l in range(N_LEVELS):
            w = jnp.exp(s_c[(2 + l) * CHUNK:(3 + l) * CHUNK])
            right = ((rowi >> l) & 1) == 1
            ql_c.append(jnp.where(right, qs_c * w, 0.0))
            kl_c.append(jnp.where(right, 0.0, k_c * w))
        ql.append(ql_c)
        kl.append(kl_c)
    att = {}
    for c, pr in inst:
        att[c, pr] = jnp.where(diag, _dot(stack(qs[rows(c), lanes(pr)]), k[rows(c), lanes(pr)], NT), 0.0)
    for l in range(N_LEVELS):
        for c, pr in inst:
            att[c, pr] = jnp.where(picks[l], _dot(stack(ql[c][l][:, lanes(pr)]), kl[c][l][:, lanes(pr)], NT),
                                   att[c, pr])
    kv = {(c, pr): jnp.where(blockdiag, _dot(v[rows(c), lanes(pr)], kd[c][:, lanes(pr)], TN), 0.0)
          for c, pr in inst}
    o_intra = {}
    for c, pr in inst:
        v_p = v[rows(c), lanes(pr)]
        o_intra[c, pr] = jnp.where(lane_a, _dot(att[c, pr][:CHUNK], v_p), _dot(att[c, pr][CHUNK:], v_p))
    st_in = {}
    for pr in range(HG_PAIRS):
        st = st_ref[pr]
        for c in range(n_chunks):
            st_in[c, pr] = st
            st = st * e_last[c][:, lanes(pr)] + kv[c, pr]
        st_ref[pr] = st
    o = jnp.concatenate(
        [jnp.concatenate([_dot(qb[c][:, lanes(pr)], st_in[c, pr], NT) + o_intra[c, pr]
                          for pr in range(HG_PAIRS)], axis=-1) for c in range(n_chunks)], axis=0)
    ms = _dot_sel_r(o * o, gm_ref[...]) * (1.0 / HD)
    o = o * lax.rsqrt(ms + EPS)
    y_ref[...] = (o * ng_ref[...] * (gt * _sigmoid(gt))).astype(y_ref.dtype)


def _hgrn_mixer(p, lb_logits, norm_g, layer, bsz, seqlen, n_chunks=MIX_BLOCK_CHUNKS):
    tb = n_chunks * CHUNK
    nb = seqlen // tb
    e = jnp.asarray(_hgrn_select_matrix(), BF16)
    gm = jnp.asarray(_head_group_matrix(HG_W), BF16)
    return pl.pallas_call(
        functools.partial(_hgrn_kernel, layer=layer, n_chunks=n_chunks),
        grid=(bsz, nb),
        in_specs=[pl.BlockSpec((tb, HG_PROJ), lambda bi, j: (bi * nb + j, 0)),
                  _const_spec((DEPTH, HG_W)), _const_spec((1, HG_W)),
                  _const_spec(e.shape), _const_spec(gm.shape)],
        out_specs=pl.BlockSpec((tb, HG_W), lambda bi, j: (bi * nb + j, 0)),
        out_shape=jax.ShapeDtypeStruct((bsz * seqlen, HG_W), BF16),
        scratch_shapes=[pltpu.VMEM((HG_PAIRS, PW, PW), F32)],
        compiler_params=pltpu.CompilerParams(dimension_semantics=("arbitrary", "arbitrary")),
        name="hgrn2_mixer",
    )(p, lb_logits, norm_g.reshape(1, HG_W), e, gm)


def _rwkv_kernel(p_ref, mu_ref, w0_ref, w2_ref, a0_ref, a2_ref, g2_ref, kk_ref, ka_ref, rk_ref,
                 lnw_ref, lnb_ref, sel_ref, gm_ref, y_ref, st_ref, prev_ref, *, n_chunks):
    tb = n_chunks * CHUNK

    @pl.when(pl.program_id(1) == 0)
    def _():
        st_ref[...] = jnp.zeros_like(st_ref)
        prev_ref[...] = jnp.zeros_like(prev_ref)

    p = p_ref[...]
    rowp = lax.broadcasted_iota(jnp.int32, p.shape, 0)
    p_prev = jnp.where(rowp == 0, prev_ref[0:1, :], pltpu.roll(p, 1, 0))
    prev_ref[0:1, :] = p[tb - 1:tb, :]
    pf = p + (p_prev - p) * mu_ref[...]

    r = pf[:, :RW_W]
    k = pf[:, RW_W:2 * RW_W]
    v = pf[:, 2 * RW_W:3 * RW_W]
    o1 = 3 * RW_W
    wd = pf[:, o1:o1 + RW_DECAY_RANK]
    ad = pf[:, o1 + RW_DECAY_RANK:o1 + RW_DECAY_RANK + RW_A_RANK]
    gd = pf[:, o1 + RW_DECAY_RANK + RW_A_RANK:]

    lw = -RW_DECAY_SCALE * _sigmoid(w0_ref[...] + _dot(jnp.tanh(wd), w2_ref[...]))
    a = _sigmoid(a0_ref[...] + _dot(ad, a2_ref[...]))
    g = _dot(_sigmoid(gd), g2_ref[...])

    gm = gm_ref[...]
    kk = k * kk_ref[...]
    kk = kk / jnp.maximum(jnp.sqrt(_dot_sel_r(kk * kk, gm)), 1e-12)
    k = k * (1.0 + (a - 1.0) * ka_ref[...])
    a_vec = -kk
    b_vec = kk * a

    sums = _dot_sel(sel_ref[...], lw)
    cw = sums[:tb]
    e_in = jnp.exp(cw)
    e_out = jnp.exp(-cw)
    e_rest = jnp.exp(sums[tb:])
    a_t = a_vec * jnp.exp(cw - lw)
    r_t = r * e_in
    b_c = b_vec * e_out
    k_c = k * e_out
    b_h = b_vec * e_rest
    k_h = k * e_rest

    lane_a, blockdiag = _pair_masks()
    ti = lax.broadcasted_iota(jnp.int32, (CHUNK, CHUNK), 0)
    si = lax.broadcasted_iota(jnp.int32, (CHUNK, CHUNK), 1)
    strict = ti > si
    incl = ti >= si
    eye = (ti == si).astype(F32)

    inst = [(c, pr) for c in range(n_chunks) for pr in range(RW_PAIRS)]
    heads = [(c, pr, hh) for c, pr in inst for hh in range(2)]
    blk = lambda t, c, pr: t[c * CHUNK:(c + 1) * CHUNK, pr * PW:(pr + 1) * PW]
    at_m, ab, ak = {}, {}, {}
    for c, pr in inst:
        at, rt = blk(a_t, c, pr), blk(r_t, c, pr)
        at_m[c, pr, 0], at_m[c, pr, 1] = jnp.where(lane_a, at, 0.0), jnp.where(lane_a, 0.0, at)
        lhs = jnp.concatenate([at_m[c, pr, 0], jnp.where(lane_a, rt, 0.0),
                               at_m[c, pr, 1], jnp.where(lane_a, 0.0, rt)], axis=0)
        ab[c, pr] = _dot(lhs, blk(b_c, c, pr), NT)
        ak[c, pr] = _dot(lhs, blk(k_c, c, pr), NT)
    npow, a_ak, a_rb, a_rk = {}, {}, {}, {}
    for c, pr, hh in heads:
        o = 2 * hh * CHUNK
        npow[c, pr, hh] = jnp.where(strict, ab[c, pr][o:o + CHUNK], 0.0)
        a_ak[c, pr, hh] = jnp.where(strict, ak[c, pr][o:o + CHUNK], 0.0)
        a_rb[c, pr, hh] = jnp.where(incl, ab[c, pr][o + CHUNK:o + 2 * CHUNK], 0.0)
        a_rk[c, pr, hh] = jnp.where(incl, ak[c, pr][o + CHUNK:o + 2 * CHUNK], 0.0)
    av = {i: _dot(a_ak[i], blk(v, i[0], i[1])) for i in heads}
    inv = {i: eye + npow[i] for i in heads}
    for _ in range(N_LEVELS - 1):
        npow = {i: _dot(npow[i], npow[i]) for i in heads}
        inv = {i: inv[i] + _dot(npow[i], inv[i]) for i in heads}
    zs = {i: _dot(inv[i], jnp.concatenate([at_m[i], av[i]], axis=1)) for i in heads}
    rz = {i: _dot(a_rb[i], zs[i]) for i in heads}
    rkv = {i: _dot(a_rk[i], blk(v, i[0], i[1])) for i in heads}
    pick = lambda d, c, pr, sl: jnp.where(lane_a, d[c, pr, 0][:, sl], d[c, pr, 1][:, sl])
    lo, hi = slice(0, PW), slice(PW, 2 * PW)
    rbar, ybar, gmat, delta = {}, {}, {}, {}
    for c, pr in inst:
        bh, kh, vv = blk(b_h, c, pr), blk(k_h, c, pr), blk(v, c, pr)
        rbar[c, pr] = blk(r_t, c, pr) + pick(rz, c, pr, lo)
        ybar[c, pr] = pick(rz, c, pr, hi) + jnp.where(lane_a, rkv[c, pr, 0], rkv[c, pr, 1])
        gmat[c, pr] = jnp.where(blockdiag, _dot(pick(zs, c, pr, lo), bh, TN), 0.0)
        delta[c, pr] = jnp.where(blockdiag, _dot(jnp.concatenate([pick(zs, c, pr, hi), vv], axis=0),
                                                 jnp.concatenate([bh, kh], axis=0), TN), 0.0)
    s = [st_ref[pr] for pr in range(RW_PAIRS)]
    ys = {}
    for c in range(n_chunks):
        for pr in range(RW_PAIRS):
            ys[c, pr] = _dot(rbar[c, pr], s[pr], NT) + ybar[c, pr]
        s = [s[pr] * blk(e_in, c, pr)[CHUNK - 1:CHUNK] + _dot(s[pr], gmat[c, pr]) + delta[c, pr]
             for pr in range(RW_PAIRS)]
    for pr in range(RW_PAIRS):
        st_ref[pr] = s[pr]
    y = jnp.concatenate([jnp.concatenate([ys[c, pr] for pr in range(RW_PAIRS)], axis=-1)
                         for c in range(n_chunks)], axis=0)

    inv_hd = 1.0 / HD
    mean = _dot_sel_r(y, gm) * inv_hd
    yc = y - mean
    var = _dot_sel_r(yc * yc, gm) * inv_hd
    yn = yc * lax.rsqrt(var + RW_GN_EPS) * lnw_ref[...] + lnb_ref[...]
    bonus = _dot_sel_r(r * k * rk_ref[...], gm) * v
    y_ref[...] = ((yn + bonus) * g).astype(y_ref.dtype)


def _rwkv_mixer(p, mu, w0, w2, a0, a2, g2, k_k, k_a, r_k, ln_w, ln_b, bsz, seqlen,
                n_chunks=MIX_BLOCK_CHUNKS):
    tb = n_chunks * CHUNK
    nb = seqlen // tb
    sel = jnp.asarray(_block_select_matrix(n_chunks), BF16)
    gm = jnp.asarray(_head_group_matrix(RW_W), BF16)
    vec = lambda t: t.reshape(1, -1)
    args = [p, vec(mu), vec(w0), w2.astype(BF16), vec(a0), a2.astype(BF16), g2.astype(BF16),
            vec(k_k), vec(k_a), vec(r_k), vec(ln_w), vec(ln_b), sel, gm]
    in_specs = [pl.BlockSpec((tb, RW_PROJ), lambda bi, j: (bi * nb + j, 0))]
    in_specs += [_const_spec(t.shape) for t in args[1:]]
    return pl.pallas_call(
        functools.partial(_rwkv_kernel, n_chunks=n_chunks),
        grid=(bsz, nb),
        in_specs=in_specs,
        out_specs=pl.BlockSpec((tb, RW_W), lambda bi, j: (bi * nb + j, 0)),
        out_shape=jax.ShapeDtypeStruct((bsz * seqlen, RW_W), BF16),
        scratch_shapes=[pltpu.VMEM((RW_PAIRS, PW, PW), F32), pltpu.VMEM((8, RW_PROJ), F32)],
        compiler_params=pltpu.CompilerParams(dimension_semantics=("arbitrary", "arbitrary"),
                                             vmem_limit_bytes=VMEM_LIMIT_BYTES),
        name="rwkv7_mixer",
    )(*args)


def _memkv_kernel(m_ref, g_ref, w_ref, k_ref, v_ref):
    mn = _rms(m_ref[...], g_ref[...]).astype(BF16)
    kv = jnp.dot(mn, w_ref[...], preferred_element_type=F32)
    k_ref[...] = kv[:, :D_MODEL].astype(BF16)
    v_ref[...] = kv[:, D_MODEL:].astype(BF16)


def _mem_kv(mem2d, g, wkv, mem_len):
    t = mem2d.shape[0]
    row = pl.BlockSpec((mem_len, D_MODEL), lambda i: (i, 0))
    return pl.pallas_call(
        _memkv_kernel,
        grid=(t // mem_len,),
        in_specs=[row, _const_spec((1, D_MODEL)), _const_spec((D_MODEL, 2 * D_MODEL))],
        out_specs=[row, row],
        out_shape=[jax.ShapeDtypeStruct((t, D_MODEL), BF16)] * 2,
        compiler_params=pltpu.CompilerParams(dimension_semantics=("arbitrary",),
                                             vmem_limit_bytes=VMEM_LIMIT_BYTES),
        name="mem_kv",
    )(mem2d, g.reshape(1, D_MODEL), wkv)


def _xattn_kernel(h_ref, yc_ref, yh_ref, yr_ref, wmo_ref, g_ref, wq_ref, mk_ref, mv_ref, wo_ref, o_ref):
    h = h_ref[...]
    h = h + jnp.dot(yc_ref[...], wmo_ref[0:CONV_W, :], preferred_element_type=F32)
    h = h + jnp.dot(yh_ref[...], wmo_ref[CONV_W:CONV_W + HG_W, :], preferred_element_type=F32)
    h = h + jnp.dot(yr_ref[...], wmo_ref[CONV_W + HG_W:, :], preferred_element_type=F32)
    hn = _rms(h, g_ref[...]).astype(BF16)
    q = jnp.dot(hn, wq_ref[...], preferred_element_type=F32).astype(BF16)
    scale = XA_HEAD_DIM ** -0.5
    outs = []
    for hd in range(XA_HEADS):
        sl = slice(hd * XA_HEAD_DIM, (hd + 1) * XA_HEAD_DIM)
        s = lax.dot_general(q[:, sl], mk_ref[:, sl], NT, preferred_element_type=F32) * scale
        pe = jnp.exp(s - jnp.max(s, axis=-1, keepdims=True))
        pr = pe / jnp.sum(pe, axis=-1, keepdims=True)
        outs.append(jnp.dot(pr.astype(BF16), mv_ref[:, sl], preferred_element_type=F32).astype(BF16))
    o = jnp.concatenate(outs, axis=-1)
    o_ref[...] = h + jnp.dot(o, wo_ref[...], preferred_element_type=F32)


def _mixout_xattn(h, yc, yh, yr, w_mix_out, g, wq, mk, mv, wo, bsz, seqlen, mem_len, tq=512):
    nq = seqlen // tq
    row = lambda n: pl.BlockSpec((tq, n), lambda bi, j: (bi * nq + j, 0))
    memspec = pl.BlockSpec((mem_len, D_MODEL), lambda bi, j: (bi, 0))
    return pl.pallas_call(
        _xattn_kernel,
        grid=(bsz, nq),
        in_specs=[row(D_MODEL), row(CONV_W), row(HG_W), row(RW_W),
                  _const_spec((D_MODEL, D_MODEL)), _const_spec((1, D_MODEL)),
                  _const_spec((D_MODEL, D_MODEL)), memspec, memspec,
                  _const_spec((D_MODEL, D_MODEL))],
        out_specs=row(D_MODEL),
        out_shape=jax.ShapeDtypeStruct((bsz * seqlen, D_MODEL), F32),
        compiler_params=pltpu.CompilerParams(dimension_semantics=("arbitrary", "arbitrary"),
                                             vmem_limit_bytes=VMEM_LIMIT_BYTES),
        name="mixout_xattn",
    )(h, yc, yh, yr, w_mix_out, g.reshape(1, D_MODEL), wq, mk, mv, wo)


def kernel(x, mem, ffn1_norm, ffn1_w_in, ffn1_w_out, mix_norm, w_mix_in, w_mix_out, conv_w, conv_b, hgrn_lb_logits, hgrn_norm, rwkv_mu, rwkv_w0, rwkv_w2, rwkv_a0, rwkv_a2, rwkv_g2, rwkv_k_k, rwkv_k_a, rwkv_r_k, rwkv_ln_w, rwkv_ln_b, xattn_norm, mem_norm, xattn_wq, xattn_wkv, xattn_wo, ffn2_norm, ffn2_w_in, ffn2_w_out, final_norm):
    bsz, seqlen, _ = x.shape
    mem_len = mem.shape[1]
    depth = ffn1_norm.shape[0]
    h = x.reshape(bsz * seqlen, D_MODEL)
    mem2d = mem.reshape(bsz * mem_len, D_MODEL)
    bf = lambda w: w.astype(BF16)
    for l in range(depth):
        h = _ffn(h, ffn1_norm[l], bf(ffn1_w_in[l]), bf(ffn1_w_out[l]))
        p_conv, p_hg, p_rw = _mix_in(h, mix_norm[l], bf(w_mix_in[l]))
        y_conv = _conv_mixer(p_conv, conv_w[l], conv_b[l], bsz, seqlen)
        y_hg = _hgrn_mixer(p_hg, hgrn_lb_logits, hgrn_norm[l], l, bsz, seqlen)
        y_rw = _rwkv_mixer(p_rw, rwkv_mu[l], rwkv_w0[l], rwkv_w2[l], rwkv_a0[l], rwkv_a2[l],
                           rwkv_g2[l], rwkv_k_k[l], rwkv_k_a[l], rwkv_r_k[l], rwkv_ln_w[l],
                           rwkv_ln_b[l], bsz, seqlen)
        mk, mv = _mem_kv(mem2d, mem_norm[l], bf(xattn_wkv[l]), mem_len)
        h = _mixout_xattn(h, y_conv, y_hg, y_rw, bf(w_mix_out[l]), xattn_norm[l], bf(xattn_wq[l]),
                          mk, mv, bf(xattn_wo[l]), bsz, seqlen, mem_len)
        h = _ffn(h, ffn2_norm[l], bf(ffn2_w_in[l]), bf(ffn2_w_out[l]),
                 final_g=final_norm if l == depth - 1 else None)
    return h.reshape(bsz, seqlen, D_MODEL)
```

```python
import functools

import numpy as np
import jax
import jax.numpy as jnp
from jax import lax
from jax.experimental import pallas as pl
from jax.experimental.pallas import tpu as pltpu

F32 = jnp.float32
BF16 = jnp.bfloat16

D_MODEL = 1024
DEPTH = 4
D_FF = 2816
FFN_RES = 0.5
EPS = 1e-6
CONV_W = 256
HD = 64
PW = 2 * HD
HG_W = 384
HG_PAIRS = HG_W // PW
HG_EXP_CLIP = 80.0
RW_W = 384
RW_PAIRS = RW_W // PW
RW_DECAY_RANK = 64
RW_A_RANK = 64
RW_G_RANK = 128
RW_PROJ = 3 * RW_W + RW_DECAY_RANK + RW_A_RANK + RW_G_RANK
RW_DECAY_SCALE = 0.606531
RW_GN_EPS = 64e-5
CONV_PROJ = 3 * CONV_W
HG_PROJ = 4 * HG_W
IN_W = CONV_PROJ + HG_PROJ + RW_PROJ
CHUNK = 64
N_LEVELS = 6
MIX_BLOCK_CHUNKS = 4
XA_HEADS = 4
XA_HEAD_DIM = D_MODEL // XA_HEADS

VMEM_LIMIT_BYTES = 56 * 1024 * 1024

NN = (((1,), (0,)), ((), ()))
NT = (((1,), (1,)), ((), ()))
TN = (((0,), (0,)), ((), ()))


def _dot(a, b, dims=NN):
    return lax.dot_general(a.astype(BF16), b.astype(BF16), dims, preferred_element_type=F32)


def _dot_sel2(e, x):
    hi = x.astype(BF16)
    lo = (x - hi.astype(F32)).astype(BF16)
    d = lambda t: lax.dot_general(e, t, NN, preferred_element_type=F32)
    return d(hi) + d(lo)


def _rms(x, g):
    return x * lax.rsqrt(jnp.mean(x * x, axis=-1, keepdims=True) + EPS) * g


def _sigmoid(x):
    return jax.nn.sigmoid(x)


def _ffn_kernel(x_ref, g_ref, win_ref, wout_ref, *rest, n_chunks, final):
    if final:
        fg_ref, o_ref = rest
    else:
        (o_ref,) = rest
    x = x_ref[...]
    xn = _rms(x, g_ref[...]).astype(BF16)
    tf = D_FF // n_chunks
    acc = jnp.zeros_like(x)
    for c in range(n_chunks):
        gate = jnp.dot(xn, win_ref[:, c * tf:(c + 1) * tf], preferred_element_type=F32)
        up = jnp.dot(xn, win_ref[:, D_FF + c * tf:D_FF + (c + 1) * tf], preferred_element_type=F32)
        act = (gate * _sigmoid(gate) * up).astype(BF16)
        acc = acc + jnp.dot(act, wout_ref[c * tf:(c + 1) * tf, :], preferred_element_type=F32)
    out = x + FFN_RES * acc
    if final:
        out = _rms(out, fg_ref[...])
    o_ref[...] = out


def _const_spec(shape):
    nd = len(shape)
    return pl.BlockSpec(shape, lambda *_: (0,) * nd)


def _layer_spec(shape, layer):
    return pl.BlockSpec((None,) + tuple(shape), lambda *_: (layer, 0, 0))


def _ffn(h, g, w_in, w_out, layer, final_g=None, tm=512, n_chunks=2):
    t = h.shape[0]
    final = final_g is not None
    in_specs = [pl.BlockSpec((tm, D_MODEL), lambda i: (i, 0)),
                _const_spec((1, D_MODEL)),
                _layer_spec((D_MODEL, 2 * D_FF), layer),
                _layer_spec((D_FF, D_MODEL), layer)]
    args = [h, g.reshape(1, D_MODEL), w_in, w_out]
    if final:
        in_specs.append(_const_spec((1, D_MODEL)))
        args.append(final_g.reshape(1, D_MODEL))
    return pl.pallas_call(
        functools.partial(_ffn_kernel, n_chunks=n_chunks, final=final),
        grid=(t // tm,),
        in_specs=in_specs,
        out_specs=pl.BlockSpec((tm, D_MODEL), lambda i: (i, 0)),
        out_shape=jax.ShapeDtypeStruct((t, D_MODEL), F32),
        compiler_params=pltpu.CompilerParams(dimension_semantics=("arbitrary",),
                                             vmem_limit_bytes=VMEM_LIMIT_BYTES),
        name="ffn_final" if final else "ffn",
    )(*args)


def _mixin_kernel(x_ref, g_ref, w_ref, pc_ref, ph_ref, pr_ref):
    xn = _rms(x_ref[...], g_ref[...]).astype(BF16)
    p = jnp.dot(xn, w_ref[...], preferred_element_type=F32)
    pc_ref[...] = p[:, :CONV_PROJ]
    ph_ref[...] = p[:, CONV_PROJ:CONV_PROJ + HG_PROJ]
    pr_ref[...] = p[:, CONV_PROJ + HG_PROJ:]


def _mix_in(h, g, w, layer, tm=512):
    t = h.shape[0]
    row = lambda n: pl.BlockSpec((tm, n), lambda i: (i, 0))
    return pl.pallas_call(
        _mixin_kernel,
        grid=(t // tm,),
        in_specs=[row(D_MODEL), _const_spec((1, D_MODEL)), _layer_spec((D_MODEL, IN_W), layer)],
        out_specs=[row(CONV_PROJ), row(HG_PROJ), row(RW_PROJ)],
        out_shape=[jax.ShapeDtypeStruct((t, CONV_PROJ), F32),
                   jax.ShapeDtypeStruct((t, HG_PROJ), F32),
                   jax.ShapeDtypeStruct((t, RW_PROJ), F32)],
        compiler_params=pltpu.CompilerParams(dimension_semantics=("arbitrary",),
                                             vmem_limit_bytes=VMEM_LIMIT_BYTES),
        name="mix_in",
    )(h, g.reshape(1, D_MODEL), w)


def _conv_kernel(p_ref, w_ref, b_ref, y_ref, carry_ref):
    @pl.when(pl.program_id(1) == 0)
    def _():
        carry_ref[...] = jnp.zeros_like(carry_ref)

    p = p_ref[...]
    tb = p.shape[0]
    b_gate = p[:, :CONV_W]
    z = p[:, CONV_W:2 * CONV_W] * p[:, 2 * CONV_W:]
    row = lax.broadcasted_iota(jnp.int32, z.shape, 0)
    c0 = carry_ref[0:1, :]
    c1 = carry_ref[1:2, :]
    z1 = jnp.where(row == 0, c1, pltpu.roll(z, 1, 0))
    z2 = jnp.where(row == 0, c0, jnp.where(row == 1, c1, pltpu.roll(z, 2, 0)))
    w = w_ref[...]
    zc = w[0:1, :] * z2 + w[1:2, :] * z1 + w[2:3, :] * z
    y_ref[...] = (b_gate * (zc + b_ref[...])).astype(y_ref.dtype)
    carry_ref[0:2, :] = z[tb - 2:tb, :]


def _conv_mixer(p, w, b, bsz, seqlen, tb=512):
    nb = seqlen // tb
    return pl.pallas_call(
        _conv_kernel,
        grid=(bsz, nb),
        in_specs=[pl.BlockSpec((tb, CONV_PROJ), lambda bi, j: (bi * nb + j, 0)),
                  _const_spec((3, CONV_W)), _const_spec((1, CONV_W))],
        out_specs=pl.BlockSpec((tb, CONV_W), lambda bi, j: (bi * nb + j, 0)),
        out_shape=jax.ShapeDtypeStruct((bsz * seqlen, CONV_W), BF16),
        scratch_shapes=[pltpu.VMEM((8, CONV_W), F32)],
        compiler_params=pltpu.CompilerParams(dimension_semantics=("arbitrary", "arbitrary")),
        name="conv_mixer",
    )(p, w, b.reshape(1, CONV_W))


def _hgrn_select_matrix():
    c = CHUNK
    e = np.zeros((1 + N_LEVELS, c, c), np.float32)
    for i in range(c):
        e[0, i, :i + 1] = 1.0
        for l in range(N_LEVELS):
            half = 1 << l
            m = (i // (2 * half)) * 2 * half + half - 1
            if i > m:
                e[1 + l, i, m + 1:i + 1] = 1.0
            else:
                e[1 + l, i, i + 1:m + 1] = 1.0
    return e.reshape((1 + N_LEVELS) * c, c)


def _head_group_matrix(width):
    idx = np.arange(width) // HD
    return (idx[:, None] == idx[None, :]).astype(np.float32)


def _block_select_matrix(n_chunks):
    i = np.arange(n_chunks * CHUNK)
    same = (i[:, None] // CHUNK) == (i[None, :] // CHUNK)
    return (same & (i[None, :] <= i[:, None])).astype(np.float32)


def _pair_masks():
    lane_a = lax.broadcasted_iota(jnp.int32, (CHUNK, PW), 1) < HD
    bi = lax.broadcasted_iota(jnp.int32, (PW, PW), 0) < HD
    bj = lax.broadcasted_iota(jnp.int32, (PW, PW), 1) < HD
    return lane_a, bi == bj


def _hgrn_kernel(p_ref, lbl_ref, ng_ref, e_ref, gm_ref, y_ref, st_ref, *, layer, n_chunks):
    @pl.when(pl.program_id(1) == 0)
    def _():
        st_ref[...] = jnp.zeros_like(st_ref)

    p = p_ref[...]
    q = p[:, :HG_W]
    z = p[:, HG_W:2 * HG_W]
    v = p[:, 2 * HG_W:3 * HG_W]
    gt = p[:, 3 * HG_W:]

    lg = lbl_ref[...]
    ex = jnp.exp(lg - jnp.max(lg, axis=0, keepdims=True))
    sm = ex / jnp.sum(ex, axis=0, keepdims=True)
    lb = jnp.maximum(jnp.sum(sm[:layer + 1], axis=0, keepdims=True) - sm[0:1], 0.0)

    log_sig = jnp.minimum(z, 0.0) - jnp.log(1.0 + jnp.exp(-jnp.abs(z)))
    lf = log_sig + jnp.log(1.0 + lb * jnp.exp(jnp.minimum(-z, HG_EXP_CLIP)))
    k = (1.0 - lb) * _sigmoid(-z)
    qs = q * _sigmoid(q)

    lane_a, blockdiag = _pair_masks()
    rowi = lax.broadcasted_iota(jnp.int32, (CHUNK, 1), 0)
    ti = lax.broadcasted_iota(jnp.int32, (CHUNK, PW), 0)
    si = lax.broadcasted_iota(jnp.int32, (CHUNK, PW), 1) & (CHUNK - 1)
    xr = ti ^ si
    lower = ti > si
    diag = ti == si
    picks = [lower & (xr >= (1 << l)) & (xr < (2 << l)) for l in range(N_LEVELS)]
    stack = lambda t: jnp.concatenate([jnp.where(lane_a, t, 0.0), jnp.where(lane_a, 0.0, t)], axis=0)

    e_mat = e_ref[...]
    inst = [(c, pr) for c in range(n_chunks) for pr in range(HG_PAIRS)]
    rows = lambda c: slice(c * CHUNK, (c + 1) * CHUNK)
    lanes = lambda pr: slice(pr * PW, (pr + 1) * PW)
    sums = [jnp.minimum(_dot_sel2(e_mat, lf[rows(c)]), 0.0) for c in range(n_chunks)]
    qb, kd, e_last, ql, kl = [], [], [], [], []
    for c in range(n_chunks):
        s_c, qs_c, k_c = sums[c], qs[rows(c)], k[rows(c)]
        b = s_c[0:CHUNK]
        b_last = b[CHUNK - 1:CHUNK]
        qb.append(qs_c * jnp.exp(b))
        kd.append(k_c * jnp.exp(jnp.minimum(b_last - b, 0.0)))
        e_last.append(jnp.exp(b_last))
        ql_c, kl_c = [], []
        for l in range(N_LEVELS):
            w = jnp.exp(s_c[(1 + l) * CHUNK:(2 + l) * CHUNK])
            right = ((rowi >> l) & 1) == 1
            ql_c.append(jnp.where(right, qs_c * w, 0.0))
            kl_c.append(jnp.where(right, 0.0, k_c * w))
        ql.append(ql_c)
        kl.append(kl_c)
    att = {}
    for c, pr in inst:
        att[c, pr] = jnp.where(diag, _dot(qs[rows(c), lanes(pr)], stack(k[rows(c), lanes(pr)]), NT), 0.0)
    for l in range(N_LEVELS):
        for c, pr in inst:
            att[c, pr] = jnp.where(picks[l], _dot(ql[c][l][:, lanes(pr)], stack(kl[c][l][:, lanes(pr)]), NT),
                                   att[c, pr])
    kv = {(c, pr): jnp.where(blockdiag, _dot(v[rows(c), lanes(pr)], kd[c][:, lanes(pr)], TN), 0.0)
          for c, pr in inst}
    o_intra = {(c, pr): _dot(att[c, pr], stack(v[rows(c), lanes(pr)])) for c, pr in inst}
    st_in = {}
    for pr in range(HG_PAIRS):
        st = st_ref[pr]
        for c in range(n_chunks):
            st_in[c, pr] = st
            st = st * e_last[c][:, lanes(pr)] + kv[c, pr]
        st_ref[pr] = st
    o = jnp.concatenate(
        [jnp.concatenate([_dot(qb[c][:, lanes(pr)], st_in[c, pr], NT) + o_intra[c, pr]
                          for pr in range(HG_PAIRS)], axis=-1) for c in range(n_chunks)], axis=0)
    ms = _dot(o * o, gm_ref[...]) * (1.0 / HD)
    o = o * lax.rsqrt(ms + EPS)
    y_ref[...] = (o * ng_ref[...] * (gt * _sigmoid(gt))).astype(y_ref.dtype)


def _hgrn_mixer(p, lb_logits, norm_g, layer, bsz, seqlen, n_chunks=MIX_BLOCK_CHUNKS):
    tb = n_chunks * CHUNK
    nb = seqlen // tb
    e = jnp.asarray(_hgrn_select_matrix(), BF16)
    gm = jnp.asarray(_head_group_matrix(HG_W), BF16)
    return pl.pallas_call(
        functools.partial(_hgrn_kernel, layer=layer, n_chunks=n_chunks),
        grid=(bsz, nb),
        in_specs=[pl.BlockSpec((tb, HG_PROJ), lambda bi, j: (bi * nb + j, 0)),
                  _const_spec((DEPTH, HG_W)), _const_spec((1, HG_W)),
                  _const_spec(e.shape), _const_spec(gm.shape)],
        out_specs=pl.BlockSpec((tb, HG_W), lambda bi, j: (bi * nb + j, 0)),
        out_shape=jax.ShapeDtypeStruct((bsz * seqlen, HG_W), BF16),
        scratch_shapes=[pltpu.VMEM((HG_PAIRS, PW, PW), F32)],
        compiler_params=pltpu.CompilerParams(dimension_semantics=("arbitrary", "arbitrary")),
        name="hgrn2_mixer",
    )(p, lb_logits, norm_g.reshape(1, HG_W), e, gm)


def _rwkv_kernel(p_ref, mu_ref, w0_ref, w2_ref, a0_ref, a2_ref, g2_ref, kk_ref, ka_ref, rk_ref,
                 lnw_ref, lnb_ref, sel_ref, gm_ref, y_ref, st_ref, prev_ref, *, n_chunks):
    tb = n_chunks * CHUNK

    @pl.when(pl.program_id(1) == 0)
    def _():
        st_ref[...] = jnp.zeros_like(st_ref)
        prev_ref[...] = jnp.zeros_like(prev_ref)

    p = p_ref[...]
    rowp = lax.broadcasted_iota(jnp.int32, p.shape, 0)
    p_prev = jnp.where(rowp == 0, prev_ref[0:1, :], pltpu.roll(p, 1, 0))
    prev_ref[0:1, :] = p[tb - 1:tb, :]
    pf = p + (p_prev - p) * mu_ref[...]

    r = pf[:, :RW_W]
    k = pf[:, RW_W:2 * RW_W]
    v = pf[:, 2 * RW_W:3 * RW_W]
    o1 = 3 * RW_W
    wd = pf[:, o1:o1 + RW_DECAY_RANK]
    ad = pf[:, o1 + RW_DECAY_RANK:o1 + RW_DECAY_RANK + RW_A_RANK]
    gd = pf[:, o1 + RW_DECAY_RANK + RW_A_RANK:]

    lw = -RW_DECAY_SCALE * _sigmoid(w0_ref[...] + _dot(jnp.tanh(wd), w2_ref[...]))
    a = _sigmoid(a0_ref[...] + _dot(ad, a2_ref[...]))
    g = _dot(_sigmoid(gd), g2_ref[...])

    gm = gm_ref[...]
    kk = k * kk_ref[...]
    kk = kk / jnp.maximum(jnp.sqrt(_dot(kk * kk, gm)), 1e-12)
    k = k * (1.0 + (a - 1.0) * ka_ref[...])
    a_vec = -kk
    b_vec = kk * a

    cw = _dot_sel2(sel_ref[...], lw)
    e_in = jnp.exp(cw)
    e_out = jnp.exp(-cw)
    e_rest = jnp.exp(jnp.concatenate(
        [cw[(c + 1) * CHUNK - 1:(c + 1) * CHUNK] - cw[c * CHUNK:(c + 1) * CHUNK] for c in range(n_chunks)],
        axis=0))
    a_t = a_vec * jnp.exp(cw - lw)
    r_t = r * e_in
    b_c = b_vec * e_out
    k_c = k * e_out
    b_h = b_vec * e_rest
    k_h = k * e_rest

    lane_a, blockdiag = _pair_masks()
    trow = lax.broadcasted_iota(jnp.int32, (CHUNK, PW), 0)
    scol = lax.broadcasted_iota(jnp.int32, (CHUNK, PW), 1) & (HD - 1)
    strict2 = trow > scol
    incl2 = trow >= scol
    eye_hi = jnp.where((trow == scol) & jnp.logical_not(lane_a), 1.0, 0.0)
    zero_blk = jnp.zeros((CHUNK, PW), F32)
    under = lambda t: jnp.concatenate([jnp.zeros_like(t), t], axis=0)
    over = lambda t: jnp.concatenate([t, jnp.zeros_like(t)], axis=0)

    inst = [(c, pr) for c in range(n_chunks) for pr in range(RW_PAIRS)]
    heads = [(c, pr, hh) for c, pr in inst for hh in range(2)]
    blk = lambda t, c, pr: t[c * CHUNK:(c + 1) * CHUNK, pr * PW:(pr + 1) * PW]
    at_m, abk = {}, {}
    for c, pr in inst:
        at, rt = blk(a_t, c, pr), blk(r_t, c, pr)
        at_m[c, pr, 0], at_m[c, pr, 1] = jnp.where(lane_a, at, 0.0), jnp.where(lane_a, 0.0, at)
        bc, kc = blk(b_c, c, pr), blk(k_c, c, pr)
        rhs = jnp.concatenate([jnp.where(lane_a, bc, 0.0), jnp.where(lane_a, kc, 0.0),
                               jnp.where(lane_a, 0.0, bc), jnp.where(lane_a, 0.0, kc)], axis=0)
        abk[c, pr] = _dot(jnp.concatenate([at, rt], axis=0), rhs, NT)
    cmat, rbk, av = {}, {}, {}
    for c, pr, hh in heads:
        hl = slice(hh * PW, (hh + 1) * PW)
        nk = jnp.where(strict2, abk[c, pr][:CHUNK, hl], 0.0)
        rbk[c, pr, hh] = jnp.where(incl2, abk[c, pr][CHUNK:, hl], 0.0)
        av[c, pr, hh] = _dot(jnp.where(lane_a, 0.0, nk), under(blk(v, c, pr)))
        cmat[c, pr, hh] = jnp.where(lane_a, nk, eye_hi)
    for _ in range(N_LEVELS):
        cmat = {i: _dot(cmat[i], over(cmat[i])) + jnp.where(lane_a, 0.0, cmat[i]) for i in heads}
    zs = {i: _dot(cmat[i], under(jnp.concatenate([at_m[i], av[i]], axis=1))) for i in heads}
    rz = {i: _dot(rbk[i], jnp.concatenate(
        [zs[i], jnp.concatenate([zero_blk, blk(v, i[0], i[1])], axis=1)], axis=0)) for i in heads}
    pick = lambda d, c, pr, sl: jnp.where(lane_a, d[c, pr, 0][:, sl], d[c, pr, 1][:, sl])
    lo, hi = slice(0, PW), slice(PW, 2 * PW)
    rbar, ybar, gmat, delta = {}, {}, {}, {}
    for c, pr in inst:
        bh, kh, vv = blk(b_h, c, pr), blk(k_h, c, pr), blk(v, c, pr)
        rbar[c, pr] = blk(r_t, c, pr) + pick(rz, c, pr, lo)
        ybar[c, pr] = pick(rz, c, pr, hi)
        gmat[c, pr] = jnp.where(blockdiag, _dot(pick(zs, c, pr, lo), bh, TN), 0.0)
        delta[c, pr] = jnp.where(blockdiag, _dot(jnp.concatenate([pick(zs, c, pr, hi), vv], axis=0),
                                                 jnp.concatenate([bh, kh], axis=0), TN), 0.0)
    s = [st_ref[pr] for pr in range(RW_PAIRS)]
    ys = {}
    for c in range(n_chunks):
        for pr in range(RW_PAIRS):
            ys[c, pr] = _dot(rbar[c, pr], s[pr], NT) + ybar[c, pr]
        s = [s[pr] * blk(e_in, c, pr)[CHUNK - 1:CHUNK] + _dot(s[pr], gmat[c, pr]) + delta[c, pr]
             for pr in range(RW_PAIRS)]
    for pr in range(RW_PAIRS):
        st_ref[pr] = s[pr]
    y = jnp.concatenate([jnp.concatenate([ys[c, pr] for pr in range(RW_PAIRS)], axis=-1)
                         for c in range(n_chunks)], axis=0)

    inv_hd = 1.0 / HD
    mean = _dot(y, gm) * inv_hd
    yc = y - mean
    var = _dot(yc * yc, gm) * inv_hd
    yn = yc * lax.rsqrt(var + RW_GN_EPS) * lnw_ref[...] + lnb_ref[...]
    bonus = _dot(r * k * rk_ref[...], gm) * v
    y_ref[...] = ((yn + bonus) * g).astype(y_ref.dtype)


def _rwkv_mixer(p, mu, w0, w2, a0, a2, g2, k_k, k_a, r_k, ln_w, ln_b, bsz, seqlen,
                n_chunks=MIX_BLOCK_CHUNKS):
    tb = n_chunks * CHUNK
    nb = seqlen // tb
    sel = jnp.asarray(_block_select_matrix(n_chunks), BF16)
    gm = jnp.asarray(_head_group_matrix(RW_W), BF16)
    vec = lambda t: t.reshape(1, -1)
    args = [p, vec(mu), vec(w0), w2.astype(BF16), vec(a0), a2.astype(BF16), g2.astype(BF16),
            vec(k_k), vec(k_a), vec(r_k), vec(ln_w), vec(ln_b), sel, gm]
    in_specs = [pl.BlockSpec((tb, RW_PROJ), lambda bi, j: (bi * nb + j, 0))]
    in_specs += [_const_spec(t.shape) for t in args[1:]]
    return pl.pallas_call(
        functools.partial(_rwkv_kernel, n_chunks=n_chunks),
        grid=(bsz, nb),
        in_specs=in_specs,
        out_specs=pl.BlockSpec((tb, RW_W), lambda bi, j: (bi * nb + j, 0)),
        out_shape=jax.ShapeDtypeStruct((bsz * seqlen, RW_W), BF16),
        scratch_shapes=[pltpu.VMEM((RW_PAIRS, PW, PW), F32), pltpu.VMEM((8, RW_PROJ), F32)],
        compiler_params=pltpu.CompilerParams(dimension_semantics=("arbitrary", "arbitrary"),
                                             vmem_limit_bytes=VMEM_LIMIT_BYTES),
        name="rwkv7_mixer",
    )(*args)


def _memkv_kernel(m_ref, g_ref, w_ref, k_ref, v_ref):
    mn = _rms(m_ref[...], g_ref[...]).astype(BF16)
    kv = jnp.dot(mn, w_ref[...], preferred_element_type=F32)
    k_ref[...] = kv[:, :D_MODEL].astype(BF16)
    v_ref[...] = kv[:, D_MODEL:].astype(BF16)


def _mem_kv(mem2d, g, wkv, layer, mem_len):
    t = mem2d.shape[0]
    row = pl.BlockSpec((mem_len, D_MODEL), lambda i: (i, 0))
    return pl.pallas_call(
        _memkv_kernel,
        grid=(t // mem_len,),
        in_specs=[row, _const_spec((1, D_MODEL)), _layer_spec((D_MODEL, 2 * D_MODEL), layer)],
        out_specs=[row, row],
        out_shape=[jax.ShapeDtypeStruct((t, D_MODEL), BF16)] * 2,
        compiler_params=pltpu.CompilerParams(dimension_semantics=("arbitrary",),
                                             vmem_limit_bytes=VMEM_LIMIT_BYTES),
        name="mem_kv",
    )(mem2d, g.reshape(1, D_MODEL), wkv)


def _xattn_kernel(h_ref, yc_ref, yh_ref, yr_ref, wmo_ref, g_ref, wq_ref, mk_ref, mv_ref, wo_ref, o_ref):
    h = h_ref[...]
    h = h + jnp.dot(yc_ref[...], wmo_ref[0:CONV_W, :], preferred_element_type=F32)
    h = h + jnp.dot(yh_ref[...], wmo_ref[CONV_W:CONV_W + HG_W, :], preferred_element_type=F32)
    h = h + jnp.dot(yr_ref[...], wmo_ref[CONV_W + HG_W:, :], preferred_element_type=F32)
    hn = _rms(h, g_ref[...]).astype(BF16)
    q = jnp.dot(hn, wq_ref[...], preferred_element_type=F32).astype(BF16)
    scale = XA_HEAD_DIM ** -0.5
    outs = []
    for hd in range(XA_HEADS):
        sl = slice(hd * XA_HEAD_DIM, (hd + 1) * XA_HEAD_DIM)
        s = lax.dot_general(q[:, sl], mk_ref[:, sl], NT, preferred_element_type=F32) * scale
        pe = jnp.exp(s - jnp.max(s, axis=-1, keepdims=True))
        pr = pe / jnp.sum(pe, axis=-1, keepdims=True)
        outs.append(jnp.dot(pr.astype(BF16), mv_ref[:, sl], preferred_element_type=F32).astype(BF16))
    o = jnp.concatenate(outs, axis=-1)
    o_ref[...] = h + jnp.dot(o, wo_ref[...], preferred_element_type=F32)


def _mixout_xattn(h, yc, yh, yr, w_mix_out, g, wq, mk, mv, wo, layer, bsz, seqlen, mem_len, tq=512):
    nq = seqlen // tq
    row = lambda n: pl.BlockSpec((tq, n), lambda bi, j: (bi * nq + j, 0))
    memspec = pl.BlockSpec((mem_len, D_MODEL), lambda bi, j: (bi, 0))
    wspec = _layer_spec((D_MODEL, D_MODEL), layer)
    return pl.pallas_call(
        _xattn_kernel,
        grid=(bsz, nq),
        in_specs=[row(D_MODEL), row(CONV_W), row(HG_W), row(RW_W),
                  wspec, _const_spec((1, D_MODEL)), wspec, memspec, memspec, wspec],
        out_specs=row(D_MODEL),
        out_shape=jax.ShapeDtypeStruct((bsz * seqlen, D_MODEL), F32),
        compiler_params=pltpu.CompilerParams(dimension_semantics=("arbitrary", "arbitrary"),
                                             vmem_limit_bytes=VMEM_LIMIT_BYTES),
        name="mixout_xattn",
    )(h, yc, yh, yr, w_mix_out, g.reshape(1, D_MODEL), wq, mk, mv, wo)


def kernel(x, mem, ffn1_norm, ffn1_w_in, ffn1_w_out, mix_norm, w_mix_in, w_mix_out, conv_w, conv_b, hgrn_lb_logits, hgrn_norm, rwkv_mu, rwkv_w0, rwkv_w2, rwkv_a0, rwkv_a2, rwkv_g2, rwkv_k_k, rwkv_k_a, rwkv_r_k, rwkv_ln_w, rwkv_ln_b, xattn_norm, mem_norm, xattn_wq, xattn_wkv, xattn_wo, ffn2_norm, ffn2_w_in, ffn2_w_out, final_norm):
    bsz, seqlen, _ = x.shape
    mem_len = mem.shape[1]
    depth = ffn1_norm.shape[0]
    h = x.reshape(bsz * seqlen, D_MODEL)
    mem2d = mem.reshape(bsz * mem_len, D_MODEL)
    (ffn1_w_in, ffn1_w_out, ffn2_w_in, ffn2_w_out, w_mix_in, w_mix_out, xattn_wq, xattn_wkv,
     xattn_wo) = (w.astype(BF16) for w in (ffn1_w_in, ffn1_w_out, ffn2_w_in, ffn2_w_out, w_mix_in,
                                           w_mix_out, xattn_wq, xattn_wkv, xattn_wo))
    for l in range(depth):
        h = _ffn(h, ffn1_norm[l], ffn1_w_in, ffn1_w_out, l)
        p_conv, p_hg, p_rw = _mix_in(h, mix_norm[l], w_mix_in, l)
        y_conv = _conv_mixer(p_conv, conv_w[l], conv_b[l], bsz, seqlen)
        y_hg = _hgrn_mixer(p_hg, hgrn_lb_logits, hgrn_norm[l], l, bsz, seqlen)
        y_rw = _rwkv_mixer(p_rw, rwkv_mu[l], rwkv_w0[l], rwkv_w2[l], rwkv_a0[l], rwkv_a2[l],
                           rwkv_g2[l], rwkv_k_k[l], rwkv_k_a[l], rwkv_r_k[l], rwkv_ln_w[l],
                           rwkv_ln_b[l], bsz, seqlen)
        mk, mv = _mem_kv(mem2d, mem_norm[l], xattn_wkv, l, mem_len)
        h = _mixout_xattn(h, y_conv, y_hg, y_rw, w_mix_out, xattn_norm[l], xattn_wq,
                          mk, mv, xattn_wo, l, bsz, seqlen, mem_len)
        h = _ffn(h, ffn2_norm[l], ffn2_w_in, ffn2_w_out, l,
                 final_g=final_norm if l == depth - 1 else None)
    return h.reshape(bsz, seqlen, D_MODEL)
```

```python
import functools

import numpy as np
import jax
import jax.numpy as jnp
from jax import lax
from jax.experimental import pallas as pl
from jax.experimental.pallas import tpu as pltpu

F32 = jnp.float32
BF16 = jnp.bfloat16

D_MODEL = 1024
DEPTH = 4
D_FF = 2816
FFN_RES = 0.5
EPS = 1e-6
CONV_W = 256
HD = 64
PW = 2 * HD
HG_W = 384
HG_PAIRS = HG_W // PW
HG_EXP_CLIP = 80.0
RW_W = 384
RW_PAIRS = RW_W // PW
RW_DECAY_RANK = 64
RW_A_RANK = 64
RW_G_RANK = 128
RW_PROJ = 3 * RW_W + RW_DECAY_RANK + RW_A_RANK + RW_G_RANK
RW_DECAY_SCALE = 0.606531
RW_GN_EPS = 64e-5
CONV_PROJ = 3 * CONV_W
HG_PROJ = 4 * HG_W
IN_W = CONV_PROJ + HG_PROJ + RW_PROJ
CHUNK = 64
N_LEVELS = 6
MIX_BLOCK_CHUNKS = 4
XA_HEADS = 4
XA_HEAD_DIM = D_MODEL // XA_HEADS

VMEM_LIMIT_BYTES = 56 * 1024 * 1024

NN = (((1,), (0,)), ((), ()))
NT = (((1,), (1,)), ((), ()))
TN = (((0,), (0,)), ((), ()))


def _dot(a, b, dims=NN):
    return lax.dot_general(a.astype(BF16), b.astype(BF16), dims, preferred_element_type=F32)


def _dot_sel2(e, x):
    hi = x.astype(BF16)
    lo = (x - hi.astype(F32)).astype(BF16)
    d = lambda t: lax.dot_general(e, t, NN, preferred_element_type=F32)
    return d(hi) + d(lo)


def _rms(x, g):
    return x * lax.rsqrt(jnp.mean(x * x, axis=-1, keepdims=True) + EPS) * g


def _sigmoid(x):
    return jax.nn.sigmoid(x)


def _ffn_kernel(x_ref, g_ref, win_ref, wout_ref, *rest, n_chunks, final):
    if final:
        fg_ref, o_ref = rest
    else:
        (o_ref,) = rest
    x = x_ref[...]
    xn = _rms(x, g_ref[...]).astype(BF16)
    tf = D_FF // n_chunks
    acc = jnp.zeros_like(x)
    for c in range(n_chunks):
        gate = jnp.dot(xn, win_ref[:, c * tf:(c + 1) * tf], preferred_element_type=F32)
        up = jnp.dot(xn, win_ref[:, D_FF + c * tf:D_FF + (c + 1) * tf], preferred_element_type=F32)
        act = (gate * _sigmoid(gate) * up).astype(BF16)
        acc = acc + jnp.dot(act, wout_ref[c * tf:(c + 1) * tf, :], preferred_element_type=F32)
    out = x + FFN_RES * acc
    if final:
        out = _rms(out, fg_ref[...])
    o_ref[...] = out


def _const_spec(shape):
    nd = len(shape)
    return pl.BlockSpec(shape, lambda *_: (0,) * nd)


def _layer_spec(shape, layer):
    return pl.BlockSpec((None,) + tuple(shape), lambda *_: (layer, 0, 0))


def _ffn(h, g, w_in, w_out, layer, final_g=None, tm=512, n_chunks=2):
    t = h.shape[0]
    final = final_g is not None
    in_specs = [pl.BlockSpec((tm, D_MODEL), lambda i: (i, 0)),
                _const_spec((1, D_MODEL)),
                _layer_spec((D_MODEL, 2 * D_FF), layer),
                _layer_spec((D_FF, D_MODEL), layer)]
    args = [h, g.reshape(1, D_MODEL), w_in, w_out]
    if final:
        in_specs.append(_const_spec((1, D_MODEL)))
        args.append(final_g.reshape(1, D_MODEL))
    return pl.pallas_call(
        functools.partial(_ffn_kernel, n_chunks=n_chunks, final=final),
        grid=(t // tm,),
        in_specs=in_specs,
        out_specs=pl.BlockSpec((tm, D_MODEL), lambda i: (i, 0)),
        out_shape=jax.ShapeDtypeStruct((t, D_MODEL), F32),
        compiler_params=pltpu.CompilerParams(dimension_semantics=("arbitrary",),
                                             vmem_limit_bytes=VMEM_LIMIT_BYTES),
        name="ffn_final" if final else "ffn",
    )(*args)


def _mixin_kernel(x_ref, g_ref, w_ref, pc_ref, ph_ref, pr_ref):
    xn = _rms(x_ref[...], g_ref[...]).astype(BF16)
    p = jnp.dot(xn, w_ref[...], preferred_element_type=F32)
    pc_ref[...] = p[:, :CONV_PROJ]
    ph_ref[...] = p[:, CONV_PROJ:CONV_PROJ + HG_PROJ]
    pr_ref[...] = p[:, CONV_PROJ + HG_PROJ:]


def _mix_in(h, g, w, layer, tm=512):
    t = h.shape[0]
    row = lambda n: pl.BlockSpec((tm, n), lambda i: (i, 0))
    return pl.pallas_call(
        _mixin_kernel,
        grid=(t // tm,),
        in_specs=[row(D_MODEL), _const_spec((1, D_MODEL)), _layer_spec((D_MODEL, IN_W), layer)],
        out_specs=[row(CONV_PROJ), row(HG_PROJ), row(RW_PROJ)],
        out_shape=[jax.ShapeDtypeStruct((t, CONV_PROJ), F32),
                   jax.ShapeDtypeStruct((t, HG_PROJ), F32),
                   jax.ShapeDtypeStruct((t, RW_PROJ), F32)],
        compiler_params=pltpu.CompilerParams(dimension_semantics=("arbitrary",),
                                             vmem_limit_bytes=VMEM_LIMIT_BYTES),
        name="mix_in",
    )(h, g.reshape(1, D_MODEL), w)


def _conv_stages(p_ref, w_ref, b_ref, y_ref, carry_ref):
    halves = 2
    tb = p_ref.shape[0]
    hb = tb // halves
    for i in range(halves):
        p = p_ref[i * hb:(i + 1) * hb, :]
        b_gate = p[:, :CONV_W]
        z = p[:, CONV_W:2 * CONV_W] * p[:, 2 * CONV_W:]
        row = lax.broadcasted_iota(jnp.int32, z.shape, 0)
        c0 = carry_ref[0:1, :]
        c1 = carry_ref[1:2, :]
        z1 = jnp.where(row == 0, c1, pltpu.roll(z, 1, 0))
        z2 = jnp.where(row == 0, c0, jnp.where(row == 1, c1, pltpu.roll(z, 2, 0)))
        w = w_ref[...]
        zc = w[0:1, :] * z2 + w[1:2, :] * z1 + w[2:3, :] * z
        y_ref[i * hb:(i + 1) * hb, 0:CONV_W] = (b_gate * (zc + b_ref[...])).astype(y_ref.dtype)
        carry_ref[0:2, :] = z[hb - 2:hb, :]
        yield


def _hgrn_select_matrix():
    c = CHUNK
    e = np.zeros((1 + N_LEVELS, c, c), np.float32)
    for i in range(c):
        e[0, i, :i + 1] = 1.0
        for l in range(N_LEVELS):
            half = 1 << l
            m = (i // (2 * half)) * 2 * half + half - 1
            if i > m:
                e[1 + l, i, m + 1:i + 1] = 1.0
            else:
                e[1 + l, i, i + 1:m + 1] = 1.0
    return e.reshape((1 + N_LEVELS) * c, c)


def _head_group_matrix(width):
    idx = np.arange(width) // HD
    return (idx[:, None] == idx[None, :]).astype(np.float32)


def _block_select_matrix(n_chunks):
    i = np.arange(n_chunks * CHUNK)
    same = (i[:, None] // CHUNK) == (i[None, :] // CHUNK)
    return (same & (i[None, :] <= i[:, None])).astype(np.float32)


def _pair_masks():
    lane_a = lax.broadcasted_iota(jnp.int32, (CHUNK, PW), 1) < HD
    bi = lax.broadcasted_iota(jnp.int32, (PW, PW), 0) < HD
    bj = lax.broadcasted_iota(jnp.int32, (PW, PW), 1) < HD
    return lane_a, bi == bj


def _hgrn_stages(p_ref, lbl_ref, ng_ref, e_ref, gm_ref, y_ref, st_ref, layer, n_chunks):
    p = p_ref[...]
    q = p[:, :HG_W]
    z = p[:, HG_W:2 * HG_W]
    v = p[:, 2 * HG_W:3 * HG_W]
    gt = p[:, 3 * HG_W:]

    lg = lbl_ref[...]
    ex = jnp.exp(lg - jnp.max(lg, axis=0, keepdims=True))
    sm = ex / jnp.sum(ex, axis=0, keepdims=True)
    lb = jnp.maximum(jnp.sum(sm[:layer + 1], axis=0, keepdims=True) - sm[0:1], 0.0)

    log_sig = jnp.minimum(z, 0.0) - jnp.log(1.0 + jnp.exp(-jnp.abs(z)))
    lf = log_sig + jnp.log(1.0 + lb * jnp.exp(jnp.minimum(-z, HG_EXP_CLIP)))
    k = (1.0 - lb) * _sigmoid(-z)
    qs = q * _sigmoid(q)
    yield

    lane_a, blockdiag = _pair_masks()
    rowi = lax.broadcasted_iota(jnp.int32, (CHUNK, 1), 0)
    ti = lax.broadcasted_iota(jnp.int32, (CHUNK, PW), 0)
    si = lax.broadcasted_iota(jnp.int32, (CHUNK, PW), 1) & (CHUNK - 1)
    xr = ti ^ si
    lower = ti > si
    diag = ti == si
    picks = [lower & (xr >= (1 << l)) & (xr < (2 << l)) for l in range(N_LEVELS)]
    stack = lambda t: jnp.concatenate([jnp.where(lane_a, t, 0.0), jnp.where(lane_a, 0.0, t)], axis=0)

    e_mat = e_ref[...]
    inst = [(c, pr) for c in range(n_chunks) for pr in range(HG_PAIRS)]
    rows = lambda c: slice(c * CHUNK, (c + 1) * CHUNK)
    lanes = lambda pr: slice(pr * PW, (pr + 1) * PW)
    sums = [jnp.minimum(_dot_sel2(e_mat, lf[rows(c)]), 0.0) for c in range(n_chunks)]
    yield
    qb, kd, e_last, ql, kl = [], [], [], [], []
    for c in range(n_chunks):
        s_c, qs_c, k_c = sums[c], qs[rows(c)], k[rows(c)]
        b = s_c[0:CHUNK]
        b_last = b[CHUNK - 1:CHUNK]
        qb.append(qs_c * jnp.exp(b))
        kd.append(k_c * jnp.exp(jnp.minimum(b_last - b, 0.0)))
        e_last.append(jnp.exp(b_last))
        ql_c, kl_c = [], []
        for l in range(N_LEVELS):
            w = jnp.exp(s_c[(1 + l) * CHUNK:(2 + l) * CHUNK])
            right = ((rowi >> l) & 1) == 1
            ql_c.append(jnp.where(right, qs_c * w, 0.0))
            kl_c.append(jnp.where(right, 0.0, k_c * w))
        ql.append(ql_c)
        kl.append(kl_c)
        yield
    att = {}
    for c, pr in inst:
        att[c, pr] = jnp.where(diag, _dot(qs[rows(c), lanes(pr)], stack(k[rows(c), lanes(pr)]), NT), 0.0)
    yield
    for l in range(N_LEVELS):
        for c, pr in inst:
            att[c, pr] = jnp.where(picks[l], _dot(ql[c][l][:, lanes(pr)], stack(kl[c][l][:, lanes(pr)]), NT),
                                   att[c, pr])
        yield
    kv = {(c, pr): jnp.where(blockdiag, _dot(v[rows(c), lanes(pr)], kd[c][:, lanes(pr)], TN), 0.0)
          for c, pr in inst}
    yield
    o_intra = {(c, pr): _dot(att[c, pr], stack(v[rows(c), lanes(pr)])) for c, pr in inst}
    yield
    st_in = {}
    for pr in range(HG_PAIRS):
        st = st_ref[pr]
        for c in range(n_chunks):
            st_in[c, pr] = st
            st = st * e_last[c][:, lanes(pr)] + kv[c, pr]
        st_ref[pr] = st
    o = jnp.concatenate(
        [jnp.concatenate([_dot(qb[c][:, lanes(pr)], st_in[c, pr], NT) + o_intra[c, pr]
                          for pr in range(HG_PAIRS)], axis=-1) for c in range(n_chunks)], axis=0)
    yield
    ms = _dot(o * o, gm_ref[...]) * (1.0 / HD)
    o = o * lax.rsqrt(ms + EPS)
    y_ref[:, CONV_W:CONV_W + HG_W] = (o * ng_ref[...] * (gt * _sigmoid(gt))).astype(y_ref.dtype)
    yield


def _rwkv_stages(p_ref, mu_ref, w0_ref, w2_ref, a0_ref, a2_ref, g2_ref, kk_ref, ka_ref, rk_ref,
                 lnw_ref, lnb_ref, sel_ref, gm_ref, y_ref, st_ref, prev_ref, n_chunks):
    tb = n_chunks * CHUNK
    p = p_ref[...]
    rowp = lax.broadcasted_iota(jnp.int32, p.shape, 0)
    p_prev = jnp.where(rowp == 0, prev_ref[0:1, :], pltpu.roll(p, 1, 0))
    prev_ref[0:1, :] = p[tb - 1:tb, :]
    pf = p + (p_prev - p) * mu_ref[...]

    r = pf[:, :RW_W]
    k = pf[:, RW_W:2 * RW_W]
    v = pf[:, 2 * RW_W:3 * RW_W]
    o1 = 3 * RW_W
    wd = pf[:, o1:o1 + RW_DECAY_RANK]
    ad = pf[:, o1 + RW_DECAY_RANK:o1 + RW_DECAY_RANK + RW_A_RANK]
    gd = pf[:, o1 + RW_DECAY_RANK + RW_A_RANK:]

    lw = -RW_DECAY_SCALE * _sigmoid(w0_ref[...] + _dot(jnp.tanh(wd), w2_ref[...]))
    a = _sigmoid(a0_ref[...] + _dot(ad, a2_ref[...]))
    g = _dot(_sigmoid(gd), g2_ref[...])
    yield

    gm = gm_ref[...]
    kk = k * kk_ref[...]
    kk = kk / jnp.maximum(jnp.sqrt(_dot(kk * kk, gm)), 1e-12)
    k = k * (1.0 + (a - 1.0) * ka_ref[...])
    a_vec = -kk
    b_vec = kk * a
    yield

    cw = _dot_sel2(sel_ref[...], lw)
    e_in = jnp.exp(cw)
    e_out = jnp.exp(-cw)
    e_rest = jnp.exp(jnp.concatenate(
        [cw[(c + 1) * CHUNK - 1:(c + 1) * CHUNK] - cw[c * CHUNK:(c + 1) * CHUNK] for c in range(n_chunks)],
        axis=0))
    a_t = a_vec * jnp.exp(cw - lw)
    r_t = r * e_in
    b_c = b_vec * e_out
    k_c = k * e_out
    b_h = b_vec * e_rest
    k_h = k * e_rest
    yield

    lane_a, blockdiag = _pair_masks()
    trow = lax.broadcasted_iota(jnp.int32, (CHUNK, PW), 0)
    scol = lax.broadcasted_iota(jnp.int32, (CHUNK, PW), 1) & (HD - 1)
    strict2 = trow > scol
    incl2 = trow >= scol
    eye_hi = jnp.where((trow == scol) & jnp.logical_not(lane_a), 1.0, 0.0)
    zero_blk = jnp.zeros((CHUNK, PW), F32)
    under = lambda t: jnp.concatenate([jnp.zeros_like(t), t], axis=0)
    over = lambda t: jnp.concatenate([t, jnp.zeros_like(t)], axis=0)

    inst = [(c, pr) for c in range(n_chunks) for pr in range(RW_PAIRS)]
    heads = [(c, pr, hh) for c, pr in inst for hh in range(2)]
    blk = lambda t, c, pr: t[c * CHUNK:(c + 1) * CHUNK, pr * PW:(pr + 1) * PW]
    at_m, abk = {}, {}
    for c, pr in inst:
        at, rt = blk(a_t, c, pr), blk(r_t, c, pr)
        at_m[c, pr, 0], at_m[c, pr, 1] = jnp.where(lane_a, at, 0.0), jnp.where(lane_a, 0.0, at)
        bc, kc = blk(b_c, c, pr), blk(k_c, c, pr)
        rhs = jnp.concatenate([jnp.where(lane_a, bc, 0.0), jnp.where(lane_a, kc, 0.0),
                               jnp.where(lane_a, 0.0, bc), jnp.where(lane_a, 0.0, kc)], axis=0)
        abk[c, pr] = _dot(jnp.concatenate([at, rt], axis=0), rhs, NT)
    yield
    cmat, rbk, av = {}, {}, {}
    for c, pr, hh in heads:
        hl = slice(hh * PW, (hh + 1) * PW)
        nk = jnp.where(strict2, abk[c, pr][:CHUNK, hl], 0.0)
        rbk[c, pr, hh] = jnp.where(incl2, abk[c, pr][CHUNK:, hl], 0.0)
        av[c, pr, hh] = _dot(jnp.where(lane_a, 0.0, nk), under(blk(v, c, pr)))
        cmat[c, pr, hh] = jnp.where(lane_a, nk, eye_hi)
    yield
    for _ in range(N_LEVELS):
        cmat = {i: _dot(cmat[i], over(cmat[i])) + jnp.where(lane_a, 0.0, cmat[i]) for i in heads}
        yield
    zs = {i: _dot(cmat[i], under(jnp.concatenate([at_m[i], av[i]], axis=1))) for i in heads}
    yield
    rz = {i: _dot(rbk[i], jnp.concatenate(
        [zs[i], jnp.concatenate([zero_blk, blk(v, i[0], i[1])], axis=1)], axis=0)) for i in heads}
    yield
    pick = lambda d, c, pr, sl: jnp.where(lane_a, d[c, pr, 0][:, sl], d[c, pr, 1][:, sl])
    lo, hi = slice(0, PW), slice(PW, 2 * PW)
    rbar, ybar, gmat, delta = {}, {}, {}, {}
    for c, pr in inst:
        bh, kh, vv = blk(b_h, c, pr), blk(k_h, c, pr), blk(v, c, pr)
        rbar[c, pr] = blk(r_t, c, pr) + pick(rz, c, pr, lo)
        ybar[c, pr] = pick(rz, c, pr, hi)
        gmat[c, pr] = jnp.where(blockdiag, _dot(pick(zs, c, pr, lo), bh, TN), 0.0)
        delta[c, pr] = jnp.where(blockdiag, _dot(jnp.concatenate([pick(zs, c, pr, hi), vv], axis=0),
                                                 jnp.concatenate([bh, kh], axis=0), TN), 0.0)
    yield
    s = [st_ref[pr] for pr in range(RW_PAIRS)]
    ys = {}
    for c in range(n_chunks):
        for pr in range(RW_PAIRS):
            ys[c, pr] = _dot(rbar[c, pr], s[pr], NT) + ybar[c, pr]
        s = [s[pr] * blk(e_in, c, pr)[CHUNK - 1:CHUNK] + _dot(s[pr], gmat[c, pr]) + delta[c, pr]
             for pr in range(RW_PAIRS)]
        yield
    for pr in range(RW_PAIRS):
        st_ref[pr] = s[pr]
    y = jnp.concatenate([jnp.concatenate([ys[c, pr] for pr in range(RW_PAIRS)], axis=-1)
                         for c in range(n_chunks)], axis=0)

    inv_hd = 1.0 / HD
    mean = _dot(y, gm) * inv_hd
    yc = y - mean
    var = _dot(yc * yc, gm) * inv_hd
    yn = yc * lax.rsqrt(var + RW_GN_EPS) * lnw_ref[...] + lnb_ref[...]
    bonus = _dot(r * k * rk_ref[...], gm) * v
    y_ref[:, CONV_W + HG_W:] = ((yn + bonus) * g).astype(y_ref.dtype)
    yield


N_CONV_IN, N_HGRN_IN, N_RWKV_IN = 3, 4, 13


def _mixers_kernel(*refs, layer, n_chunks):
    it = iter(refs)
    take = lambda n: [next(it) for _ in range(n)]
    conv_in, hgrn_in, rwkv_in = take(N_CONV_IN), take(N_HGRN_IN), take(N_RWKV_IN)
    gm_ref, y_ref, conv_carry, hg_state, rw_state, rw_prev = take(6)

    @pl.when(pl.program_id(1) == 0)
    def _():
        for ref in (conv_carry, hg_state, rw_state, rw_prev):
            ref[...] = jnp.zeros_like(ref)

    stages = [
        _rwkv_stages(*rwkv_in, gm_ref, y_ref, rw_state, rw_prev, n_chunks),
        _hgrn_stages(*hgrn_in, gm_ref, y_ref, hg_state, layer, n_chunks),
        _conv_stages(*conv_in, y_ref, conv_carry),
    ]
    while stages:
        for gen in list(stages):
            if next(gen, stages) is stages:
                stages.remove(gen)


def _mixers(p_conv, p_hg, p_rw, conv_w, conv_b, lb_logits, hg_norm, mu, w0, w2, a0, a2, g2, k_k, k_a,
            r_k, ln_w, ln_b, layer, bsz, seqlen, n_chunks=MIX_BLOCK_CHUNKS):
    tb = n_chunks * CHUNK
    nb = seqlen // tb
    vec = lambda t: t.reshape(1, -1)
    row = lambda n: pl.BlockSpec((tb, n), lambda bi, j: (bi * nb + j, 0))
    conv_args = [p_conv, conv_w, vec(conv_b)]
    hgrn_args = [p_hg, lb_logits, vec(hg_norm), jnp.asarray(_hgrn_select_matrix(), BF16)]
    rwkv_args = [p_rw, vec(mu), vec(w0), w2.astype(BF16), vec(a0), a2.astype(BF16), g2.astype(BF16),
                 vec(k_k), vec(k_a), vec(r_k), vec(ln_w), vec(ln_b),
                 jnp.asarray(_block_select_matrix(n_chunks), BF16)]
    assert (len(conv_args), len(hgrn_args), len(rwkv_args)) == (N_CONV_IN, N_HGRN_IN, N_RWKV_IN)
    args = conv_args + hgrn_args + rwkv_args + [jnp.asarray(_head_group_matrix(HG_W), BF16)]
    streamed = {0: CONV_PROJ, N_CONV_IN: HG_PROJ, N_CONV_IN + N_HGRN_IN: RW_PROJ}
    in_specs = [row(streamed[i]) if i in streamed else _const_spec(t.shape) for i, t in enumerate(args)]
    mix_w = CONV_W + HG_W + RW_W
    return pl.pallas_call(
        functools.partial(_mixers_kernel, layer=layer, n_chunks=n_chunks),
        grid=(bsz, nb),
        in_specs=in_specs,
        out_specs=row(mix_w),
        out_shape=jax.ShapeDtypeStruct((bsz * seqlen, mix_w), BF16),
        scratch_shapes=[pltpu.VMEM((8, CONV_W), F32), pltpu.VMEM((HG_PAIRS, PW, PW), F32),
                        pltpu.VMEM((RW_PAIRS, PW, PW), F32), pltpu.VMEM((8, RW_PROJ), F32)],
        compiler_params=pltpu.CompilerParams(dimension_semantics=("arbitrary", "arbitrary"),
                                             vmem_limit_bytes=VMEM_LIMIT_BYTES),
        name="mixers",
    )(*args)


def _memkv_kernel(m_ref, g_ref, w_ref, k_ref, v_ref):
    mn = _rms(m_ref[...], g_ref[...]).astype(BF16)
    kv = jnp.dot(mn, w_ref[...], preferred_element_type=F32)
    k_ref[...] = kv[:, :D_MODEL].astype(BF16)
    v_ref[...] = kv[:, D_MODEL:].astype(BF16)


def _mem_kv(mem2d, g, wkv, layer, mem_len):
    t = mem2d.shape[0]
    row = pl.BlockSpec((mem_len, D_MODEL), lambda i: (i, 0))
    return pl.pallas_call(
        _memkv_kernel,
        grid=(t // mem_len,),
        in_specs=[row, _const_spec((1, D_MODEL)), _layer_spec((D_MODEL, 2 * D_MODEL), layer)],
        out_specs=[row, row],
        out_shape=[jax.ShapeDtypeStruct((t, D_MODEL), BF16)] * 2,
        compiler_params=pltpu.CompilerParams(dimension_semantics=("arbitrary",),
                                             vmem_limit_bytes=VMEM_LIMIT_BYTES),
        name="mem_kv",
    )(mem2d, g.reshape(1, D_MODEL), wkv)


def _xattn_kernel(h_ref, y_ref, wmo_ref, g_ref, wq_ref, mk_ref, mv_ref, wo_ref, o_ref):
    h = h_ref[...] + jnp.dot(y_ref[...], wmo_ref[...], preferred_element_type=F32)
    hn = _rms(h, g_ref[...]).astype(BF16)
    q = jnp.dot(hn, wq_ref[...], preferred_element_type=F32).astype(BF16)
    scale = XA_HEAD_DIM ** -0.5
    outs = []
    for hd in range(XA_HEADS):
        sl = slice(hd * XA_HEAD_DIM, (hd + 1) * XA_HEAD_DIM)
        s = lax.dot_general(q[:, sl], mk_ref[:, sl], NT, preferred_element_type=F32) * scale
        pe = jnp.exp(s - jnp.max(s, axis=-1, keepdims=True))
        pr = pe / jnp.sum(pe, axis=-1, keepdims=True)
        outs.append(jnp.dot(pr.astype(BF16), mv_ref[:, sl], preferred_element_type=F32).astype(BF16))
    o = jnp.concatenate(outs, axis=-1)
    o_ref[...] = h + jnp.dot(o, wo_ref[...], preferred_element_type=F32)


def _mixout_xattn(h, y, w_mix_out, g, wq, mk, mv, wo, layer, bsz, seqlen, mem_len, tq=512):
    nq = seqlen // tq
    row = lambda n: pl.BlockSpec((tq, n), lambda bi, j: (bi * nq + j, 0))
    memspec = pl.BlockSpec((mem_len, D_MODEL), lambda bi, j: (bi, 0))
    wspec = _layer_spec((D_MODEL, D_MODEL), layer)
    return pl.pallas_call(
        _xattn_kernel,
        grid=(bsz, nq),
        in_specs=[row(D_MODEL), row(D_MODEL),
                  wspec, _const_spec((1, D_MODEL)), wspec, memspec, memspec, wspec],
        out_specs=row(D_MODEL),
        out_shape=jax.ShapeDtypeStruct((bsz * seqlen, D_MODEL), F32),
        compiler_params=pltpu.CompilerParams(dimension_semantics=("arbitrary", "arbitrary"),
                                             vmem_limit_bytes=VMEM_LIMIT_BYTES),
        name="mixout_xattn",
    )(h, y, w_mix_out, g.reshape(1, D_MODEL), wq, mk, mv, wo)


def kernel(x, mem, ffn1_norm, ffn1_w_in, ffn1_w_out, mix_norm, w_mix_in, w_mix_out, conv_w, conv_b, hgrn_lb_logits, hgrn_norm, rwkv_mu, rwkv_w0, rwkv_w2, rwkv_a0, rwkv_a2, rwkv_g2, rwkv_k_k, rwkv_k_a, rwkv_r_k, rwkv_ln_w, rwkv_ln_b, xattn_norm, mem_norm, xattn_wq, xattn_wkv, xattn_wo, ffn2_norm, ffn2_w_in, ffn2_w_out, final_norm):
    bsz, seqlen, _ = x.shape
    mem_len = mem.shape[1]
    depth = ffn1_norm.shape[0]
    h = x.reshape(bsz * seqlen, D_MODEL)
    mem2d = mem.reshape(bsz * mem_len, D_MODEL)
    (ffn1_w_in, ffn1_w_out, ffn2_w_in, ffn2_w_out, w_mix_in, w_mix_out, xattn_wq, xattn_wkv,
     xattn_wo) = (w.astype(BF16) for w in (ffn1_w_in, ffn1_w_out, ffn2_w_in, ffn2_w_out, w_mix_in,
                                           w_mix_out, xattn_wq, xattn_wkv, xattn_wo))
    for l in range(depth):
        h = _ffn(h, ffn1_norm[l], ffn1_w_in, ffn1_w_out, l)
        p_conv, p_hg, p_rw = _mix_in(h, mix_norm[l], w_mix_in, l)
        y = _mixers(p_conv, p_hg, p_rw, conv_w[l], conv_b[l], hgrn_lb_logits, hgrn_norm[l],
                    rwkv_mu[l], rwkv_w0[l], rwkv_w2[l], rwkv_a0[l], rwkv_a2[l], rwkv_g2[l],
                    rwkv_k_k[l], rwkv_k_a[l], rwkv_r_k[l], rwkv_ln_w[l], rwkv_ln_b[l], l, bsz, seqlen)
        mk, mv = _mem_kv(mem2d, mem_norm[l], xattn_wkv, l, mem_len)
        h = _mixout_xattn(h, y, w_mix_out, xattn_norm[l], xattn_wq,
                          mk, mv, xattn_wo, l, bsz, seqlen, mem_len)
        h = _ffn(h, ffn2_norm[l], ffn2_w_in, ffn2_w_out, l,
                 final_g=final_norm if l == depth - 1 else None)
    return h.reshape(bsz, seqlen, D_MODEL)
```

```python
import functools

import numpy as np
import jax
import jax.numpy as jnp
from jax import lax
from jax.experimental import pallas as pl
from jax.experimental.pallas import tpu as pltpu

F32 = jnp.float32
BF16 = jnp.bfloat16

D_MODEL = 1024
DEPTH = 4
D_FF = 2816
FFN_RES = 0.5
EPS = 1e-6
CONV_W = 256
HD = 64
PW = 2 * HD
HG_W = 384
HG_PAIRS = HG_W // PW
HG_EXP_CLIP = 80.0
RW_W = 384
RW_PAIRS = RW_W // PW
RW_DECAY_RANK = 64
RW_A_RANK = 64
RW_G_RANK = 128
RW_PROJ = 3 * RW_W + RW_DECAY_RANK + RW_A_RANK + RW_G_RANK
RW_DECAY_SCALE = 0.606531
RW_GN_EPS = 64e-5
CONV_PROJ = 3 * CONV_W
HG_PROJ = 4 * HG_W
IN_W = CONV_PROJ + HG_PROJ + RW_PROJ
CHUNK = 64
N_LEVELS = 6
MIX_BLOCK_CHUNKS = 4
XA_HEADS = 4
XA_HEAD_DIM = D_MODEL // XA_HEADS

VMEM_LIMIT_BYTES = 56 * 1024 * 1024
MXU_TILE = 256
FFN_SPLITS = (0, 6 * MXU_TILE, D_FF)
assert all(b % MXU_TILE == 0 for b in FFN_SPLITS)

NN = (((1,), (0,)), ((), ()))
NT = (((1,), (1,)), ((), ()))
TN = (((0,), (0,)), ((), ()))


def _dot(a, b, dims=NN):
    return lax.dot_general(a.astype(BF16), b.astype(BF16), dims, preferred_element_type=F32)


def _dot_sel2(e, x):
    hi = x.astype(BF16)
    lo = (x - hi.astype(F32)).astype(BF16)
    d = lambda t: lax.dot_general(e, t, NN, preferred_element_type=F32)
    return d(hi) + d(lo)


def _rms(x, g):
    return x * lax.rsqrt(jnp.mean(x * x, axis=-1, keepdims=True) + EPS) * g


def _sigmoid(x):
    return jax.nn.sigmoid(x)


def _ffn_kernel(x_ref, g_ref, win_ref, wout_ref, *rest, final):
    if final:
        fg_ref, o_ref = rest
    else:
        (o_ref,) = rest
    x = x_ref[...]
    xn = _rms(x, g_ref[...]).astype(BF16)
    acc = jnp.zeros_like(x)
    for lo, hi in zip(FFN_SPLITS[:-1], FFN_SPLITS[1:]):
        gate = jnp.dot(xn, win_ref[:, lo:hi], preferred_element_type=F32)
        up = jnp.dot(xn, win_ref[:, D_FF + lo:D_FF + hi], preferred_element_type=F32)
        act = (gate * _sigmoid(gate) * up).astype(BF16)
        acc = acc + jnp.dot(act, wout_ref[lo:hi, :], preferred_element_type=F32)
    out = x + FFN_RES * acc
    if final:
        out = _rms(out, fg_ref[...])
    o_ref[...] = out


def _const_spec(shape):
    nd = len(shape)
    return pl.BlockSpec(shape, lambda *_: (0,) * nd)


def _layer_spec(shape, layer):
    return pl.BlockSpec((None,) + tuple(shape), lambda *_: (layer, 0, 0))


def _ffn(h, g, w_in, w_out, layer, final_g=None, tm=512):
    t = h.shape[0]
    final = final_g is not None
    in_specs = [pl.BlockSpec((tm, D_MODEL), lambda i: (i, 0)),
                _const_spec((1, D_MODEL)),
                _layer_spec((D_MODEL, 2 * D_FF), layer),
                _layer_spec((D_FF, D_MODEL), layer)]
    args = [h, g.reshape(1, D_MODEL), w_in, w_out]
    if final:
        in_specs.append(_const_spec((1, D_MODEL)))
        args.append(final_g.reshape(1, D_MODEL))
    return pl.pallas_call(
        functools.partial(_ffn_kernel, final=final),
        grid=(t // tm,),
        in_specs=in_specs,
        out_specs=pl.BlockSpec((tm, D_MODEL), lambda i: (i, 0)),
        out_shape=jax.ShapeDtypeStruct((t, D_MODEL), F32),
        compiler_params=pltpu.CompilerParams(dimension_semantics=("arbitrary",),
                                             vmem_limit_bytes=VMEM_LIMIT_BYTES),
        name="ffn_final" if final else "ffn",
    )(*args)


def _mixin_kernel(x_ref, g_ref, w_ref, pc_ref, ph_ref, pr_ref):
    xn = _rms(x_ref[...], g_ref[...]).astype(BF16)
    p = jnp.dot(xn, w_ref[...], preferred_element_type=F32)
    pc_ref[...] = p[:, :CONV_PROJ]
    ph_ref[...] = p[:, CONV_PROJ:CONV_PROJ + HG_PROJ]
    pr_ref[...] = p[:, CONV_PROJ + HG_PROJ:]


def _mix_in(h, g, w, layer, tm=512):
    t = h.shape[0]
    row = lambda n: pl.BlockSpec((tm, n), lambda i: (i, 0))
    return pl.pallas_call(
        _mixin_kernel,
        grid=(t // tm,),
        in_specs=[row(D_MODEL), _const_spec((1, D_MODEL)), _layer_spec((D_MODEL, IN_W), layer)],
        out_specs=[row(CONV_PROJ), row(HG_PROJ), row(RW_PROJ)],
        out_shape=[jax.ShapeDtypeStruct((t, CONV_PROJ), F32),
                   jax.ShapeDtypeStruct((t, HG_PROJ), F32),
                   jax.ShapeDtypeStruct((t, RW_PROJ), F32)],
        compiler_params=pltpu.CompilerParams(dimension_semantics=("arbitrary",),
                                             vmem_limit_bytes=VMEM_LIMIT_BYTES),
        name="mix_in",
    )(h, g.reshape(1, D_MODEL), w)


def _conv_stages(p_ref, w_ref, b_ref, y_ref, carry_ref):
    halves = 2
    tb = p_ref.shape[0]
    hb = tb // halves
    for i in range(halves):
        p = p_ref[i * hb:(i + 1) * hb, :]
        b_gate = p[:, :CONV_W]
        z = p[:, CONV_W:2 * CONV_W] * p[:, 2 * CONV_W:]
        row = lax.broadcasted_iota(jnp.int32, z.shape, 0)
        c0 = carry_ref[0:1, :]
        c1 = carry_ref[1:2, :]
        z1 = jnp.where(row == 0, c1, pltpu.roll(z, 1, 0))
        z2 = jnp.where(row == 0, c0, jnp.where(row == 1, c1, pltpu.roll(z, 2, 0)))
        w = w_ref[...]
        zc = w[0:1, :] * z2 + w[1:2, :] * z1 + w[2:3, :] * z
        y_ref[i * hb:(i + 1) * hb, 0:CONV_W] = (b_gate * (zc + b_ref[...])).astype(y_ref.dtype)
        carry_ref[0:2, :] = z[hb - 2:hb, :]
        yield


def _hgrn_select_matrix():
    c = CHUNK
    e = np.zeros((1 + N_LEVELS, c, c), np.float32)
    for i in range(c):
        e[0, i, :i + 1] = 1.0
        for l in range(N_LEVELS):
            half = 1 << l
            m = (i // (2 * half)) * 2 * half + half - 1
            if i > m:
                e[1 + l, i, m + 1:i + 1] = 1.0
            else:
                e[1 + l, i, i + 1:m + 1] = 1.0
    return e.reshape((1 + N_LEVELS) * c, c)


def _head_group_matrix(width):
    idx = np.arange(width) // HD
    return (idx[:, None] == idx[None, :]).astype(np.float32)


def _block_select_matrix(n_chunks):
    i = np.arange(n_chunks * CHUNK)
    same = (i[:, None] // CHUNK) == (i[None, :] // CHUNK)
    return (same & (i[None, :] <= i[:, None])).astype(np.float32)


def _pair_masks():
    lane_a = lax.broadcasted_iota(jnp.int32, (CHUNK, PW), 1) < HD
    bi = lax.broadcasted_iota(jnp.int32, (PW, PW), 0) < HD
    bj = lax.broadcasted_iota(jnp.int32, (PW, PW), 1) < HD
    return lane_a, bi == bj


def _hgrn_stages(p_ref, lbl_ref, ng_ref, e_ref, gm_ref, y_ref, st_ref, layer, n_chunks):
    p = p_ref[...]
    q = p[:, :HG_W]
    z = p[:, HG_W:2 * HG_W]
    v = p[:, 2 * HG_W:3 * HG_W]
    gt = p[:, 3 * HG_W:]

    lg = lbl_ref[...]
    ex = jnp.exp(lg - jnp.max(lg, axis=0, keepdims=True))
    sm = ex / jnp.sum(ex, axis=0, keepdims=True)
    lb = jnp.maximum(jnp.sum(sm[:layer + 1], axis=0, keepdims=True) - sm[0:1], 0.0)

    log_sig = jnp.minimum(z, 0.0) - jnp.log(1.0 + jnp.exp(-jnp.abs(z)))
    lf = log_sig + jnp.log(1.0 + lb * jnp.exp(jnp.minimum(-z, HG_EXP_CLIP)))
    k = (1.0 - lb) * _sigmoid(-z)
    qs = q * _sigmoid(q)
    yield

    lane_a, blockdiag = _pair_masks()
    rowi = lax.broadcasted_iota(jnp.int32, (CHUNK, 1), 0)
    ti = lax.broadcasted_iota(jnp.int32, (CHUNK, PW), 0)
    si = lax.broadcasted_iota(jnp.int32, (CHUNK, PW), 1) & (CHUNK - 1)
    xr = ti ^ si
    lower = ti > si
    diag = ti == si
    picks = [lower & (xr >= (1 << l)) & (xr < (2 << l)) for l in range(N_LEVELS)]
    zero_b = jnp.zeros((CHUNK, PW), BF16)
    stack = lambda t: jnp.concatenate([jnp.where(lane_a, t, zero_b), jnp.where(lane_a, zero_b, t)], axis=0)
    k_b, v_b = k.astype(BF16), v.astype(BF16)

    e_mat = e_ref[...]
    inst = [(c, pr) for c in range(n_chunks) for pr in range(HG_PAIRS)]
    rows = lambda c: slice(c * CHUNK, (c + 1) * CHUNK)
    lanes = lambda pr: slice(pr * PW, (pr + 1) * PW)
    sums = [jnp.minimum(_dot_sel2(e_mat, lf[rows(c)]), 0.0) for c in range(n_chunks)]
    yield
    qb, kd, e_last, ql, kl = [], [], [], [], []
    for c in range(n_chunks):
        s_c, qs_c, k_c = sums[c], qs[rows(c)], k[rows(c)]
        b = s_c[0:CHUNK]
        b_last = b[CHUNK - 1:CHUNK]
        qb.append(qs_c * jnp.exp(b))
        kd.append(k_c * jnp.exp(jnp.minimum(b_last - b, 0.0)))
        e_last.append(jnp.exp(b_last))
        ql_c, kl_c = [], []
        for l in range(N_LEVELS):
            w = jnp.exp(s_c[(1 + l) * CHUNK:(2 + l) * CHUNK])
            right = ((rowi >> l) & 1) == 1
            ql_c.append(jnp.where(right, qs_c * w, 0.0).astype(BF16))
            kl_c.append(jnp.where(right, 0.0, k_c * w).astype(BF16))
        ql.append(ql_c)
        kl.append(kl_c)
        yield
    att = {}
    for c, pr in inst:
        att[c, pr] = jnp.where(diag, _dot(qs[rows(c), lanes(pr)], stack(k_b[rows(c), lanes(pr)]), NT), 0.0)
    yield
    for l in range(N_LEVELS):
        for c, pr in inst:
            att[c, pr] = jnp.where(picks[l], _dot(ql[c][l][:, lanes(pr)], stack(kl[c][l][:, lanes(pr)]), NT),
                                   att[c, pr])
        yield
    kv = {(c, pr): jnp.where(blockdiag, _dot(v_b[rows(c), lanes(pr)], kd[c][:, lanes(pr)], TN), 0.0)
          for c, pr in inst}
    yield
    o_intra = {(c, pr): _dot(att[c, pr], stack(v_b[rows(c), lanes(pr)])) for c, pr in inst}
    yield
    st_in = {}
    for pr in range(HG_PAIRS):
        st = st_ref[pr]
        for c in range(n_chunks):
            st_in[c, pr] = st
            st = st * e_last[c][:, lanes(pr)] + kv[c, pr]
        st_ref[pr] = st
    o = jnp.concatenate(
        [jnp.concatenate([_dot(qb[c][:, lanes(pr)], st_in[c, pr], NT) + o_intra[c, pr]
                          for pr in range(HG_PAIRS)], axis=-1) for c in range(n_chunks)], axis=0)
    yield
    ms = _dot(o * o, gm_ref[...]) * (1.0 / HD)
    o = o * lax.rsqrt(ms + EPS)
    y_ref[:, CONV_W:CONV_W + HG_W] = (o * ng_ref[...] * (gt * _sigmoid(gt))).astype(y_ref.dtype)
    yield


def _rwkv_stages(p_ref, mu_ref, w0_ref, w2_ref, a0_ref, a2_ref, g2_ref, kk_ref, ka_ref, rk_ref,
                 lnw_ref, lnb_ref, sel_ref, gm_ref, y_ref, st_ref, prev_ref, n_chunks):
    tb = n_chunks * CHUNK
    p = p_ref[...]
    rowp = lax.broadcasted_iota(jnp.int32, p.shape, 0)
    p_prev = jnp.where(rowp == 0, prev_ref[0:1, :], pltpu.roll(p, 1, 0))
    prev_ref[0:1, :] = p[tb - 1:tb, :]
    pf = p + (p_prev - p) * mu_ref[...]

    r = pf[:, :RW_W]
    k = pf[:, RW_W:2 * RW_W]
    v = pf[:, 2 * RW_W:3 * RW_W]
    o1 = 3 * RW_W
    wd = pf[:, o1:o1 + RW_DECAY_RANK]
    ad = pf[:, o1 + RW_DECAY_RANK:o1 + RW_DECAY_RANK + RW_A_RANK]
    gd = pf[:, o1 + RW_DECAY_RANK + RW_A_RANK:]

    lw = -RW_DECAY_SCALE * _sigmoid(w0_ref[...] + _dot(jnp.tanh(wd), w2_ref[...]))
    a = _sigmoid(a0_ref[...] + _dot(ad, a2_ref[...]))
    g = _dot(_sigmoid(gd), g2_ref[...])
    yield

    gm = gm_ref[...]
    kk = k * kk_ref[...]
    kk = kk / jnp.maximum(jnp.sqrt(_dot(kk * kk, gm)), 1e-12)
    k = k * (1.0 + (a - 1.0) * ka_ref[...])
    a_vec = -kk
    b_vec = kk * a
    yield

    cw = _dot_sel2(sel_ref[...], lw)
    e_in = jnp.exp(cw)
    e_out = jnp.exp(-cw)
    e_rest = jnp.exp(jnp.concatenate(
        [cw[(c + 1) * CHUNK - 1:(c + 1) * CHUNK] - cw[c * CHUNK:(c + 1) * CHUNK] for c in range(n_chunks)],
        axis=0))
    a_t = a_vec * jnp.exp(cw - lw)
    r_t = r * e_in
    b_c = b_vec * e_out
    k_c = k * e_out
    b_h = b_vec * e_rest
    k_h = k * e_rest
    yield

    lane_a, blockdiag = _pair_masks()
    trow = lax.broadcasted_iota(jnp.int32, (CHUNK, PW), 0)
    scol = lax.broadcasted_iota(jnp.int32, (CHUNK, PW), 1) & (HD - 1)
    strict2 = trow > scol
    incl2 = trow >= scol
    eye_hi = jnp.where((trow == scol) & jnp.logical_not(lane_a), 1.0, 0.0)
    zero_blk = jnp.zeros((CHUNK, PW), BF16)
    under = lambda t: jnp.concatenate([jnp.zeros_like(t), t], axis=0)
    over = lambda t: jnp.concatenate([t, jnp.zeros_like(t)], axis=0)

    inst = [(c, pr) for c in range(n_chunks) for pr in range(RW_PAIRS)]
    heads = [(c, pr, hh) for c, pr in inst for hh in range(2)]
    blk = lambda t, c, pr: t[c * CHUNK:(c + 1) * CHUNK, pr * PW:(pr + 1) * PW]
    v_b = v.astype(BF16)
    at_m, abk = {}, {}
    for c, pr in inst:
        at, rt = blk(a_t, c, pr), blk(r_t, c, pr)
        at_m[c, pr, 0], at_m[c, pr, 1] = jnp.where(lane_a, at, 0.0), jnp.where(lane_a, 0.0, at)
        bc, kc = blk(b_c, c, pr), blk(k_c, c, pr)
        rhs = jnp.concatenate([jnp.where(lane_a, bc, 0.0), jnp.where(lane_a, kc, 0.0),
                               jnp.where(lane_a, 0.0, bc), jnp.where(lane_a, 0.0, kc)], axis=0)
        abk[c, pr] = _dot(jnp.concatenate([at, rt], axis=0), rhs, NT)
    yield
    cmat, rbk, av = {}, {}, {}
    for c, pr, hh in heads:
        hl = slice(hh * PW, (hh + 1) * PW)
        nk = jnp.where(strict2, abk[c, pr][:CHUNK, hl], 0.0)
        rbk[c, pr, hh] = jnp.where(incl2, abk[c, pr][CHUNK:, hl], 0.0)
        av[c, pr, hh] = _dot(jnp.where(lane_a, 0.0, nk), under(blk(v_b, c, pr)))
        cmat[c, pr, hh] = jnp.where(lane_a, nk, eye_hi)
    yield
    for _ in range(N_LEVELS):
        cb = {i: cmat[i].astype(BF16) for i in heads}
        cmat = {i: _dot(cb[i], over(cb[i])) + jnp.where(lane_a, 0.0, cmat[i]) for i in heads}
        yield
    zs = {i: _dot(cmat[i], under(jnp.concatenate([at_m[i], av[i]], axis=1))) for i in heads}
    yield
    rz = {i: _dot(rbk[i], jnp.concatenate(
        [zs[i].astype(BF16), jnp.concatenate([zero_blk, blk(v_b, i[0], i[1])], axis=1)], axis=0)) for i in heads}
    yield
    pick = lambda d, c, pr, sl: jnp.where(lane_a, d[c, pr, 0][:, sl], d[c, pr, 1][:, sl])
    lo, hi = slice(0, PW), slice(PW, 2 * PW)
    rbar, ybar, gmat, delta = {}, {}, {}, {}
    for c, pr in inst:
        bh, kh, vv = blk(b_h, c, pr).astype(BF16), blk(k_h, c, pr).astype(BF16), blk(v_b, c, pr)
        rbar[c, pr] = blk(r_t, c, pr) + pick(rz, c, pr, lo)
        ybar[c, pr] = pick(rz, c, pr, hi)
        gmat[c, pr] = jnp.where(blockdiag, _dot(pick(zs, c, pr, lo), bh, TN), 0.0)
        delta[c, pr] = jnp.where(blockdiag, _dot(jnp.concatenate([pick(zs, c, pr, hi).astype(BF16), vv], axis=0),
                                                 jnp.concatenate([bh, kh], axis=0), TN), 0.0)
    yield
    s = [st_ref[pr] for pr in range(RW_PAIRS)]
    ys = {}
    for c in range(n_chunks):
        for pr in range(RW_PAIRS):
            ys[c, pr] = _dot(rbar[c, pr], s[pr], NT) + ybar[c, pr]
        s = [s[pr] * blk(e_in, c, pr)[CHUNK - 1:CHUNK] + _dot(s[pr], gmat[c, pr]) + delta[c, pr]
             for pr in range(RW_PAIRS)]
        yield
    for pr in range(RW_PAIRS):
        st_ref[pr] = s[pr]
    y = jnp.concatenate([jnp.concatenate([ys[c, pr] for pr in range(RW_PAIRS)], axis=-1)
                         for c in range(n_chunks)], axis=0)

    inv_hd = 1.0 / HD
    mean = _dot(y, gm) * inv_hd
    yc = y - mean
    var = _dot(yc * yc, gm) * inv_hd
    yn = yc * lax.rsqrt(var + RW_GN_EPS) * lnw_ref[...] + lnb_ref[...]
    bonus = _dot(r * k * rk_ref[...], gm) * v
    y_ref[:, CONV_W + HG_W:] = ((yn + bonus) * g).astype(y_ref.dtype)
    yield


N_CONV_IN, N_HGRN_IN, N_RWKV_IN = 3, 4, 13


def _mixers_kernel(*refs, layer, n_chunks):
    it = iter(refs)
    take = lambda n: [next(it) for _ in range(n)]
    conv_in, hgrn_in, rwkv_in = take(N_CONV_IN), take(N_HGRN_IN), take(N_RWKV_IN)
    gm_ref, y_ref, conv_carry, hg_state, rw_state, rw_prev = take(6)

    @pl.when(pl.program_id(1) == 0)
    def _():
        for ref in (conv_carry, hg_state, rw_state, rw_prev):
            ref[...] = jnp.zeros_like(ref)

    stages = [
        _rwkv_stages(*rwkv_in, gm_ref, y_ref, rw_state, rw_prev, n_chunks),
        _hgrn_stages(*hgrn_in, gm_ref, y_ref, hg_state, layer, n_chunks),
        _conv_stages(*conv_in, y_ref, conv_carry),
    ]
    while stages:
        for gen in list(stages):
            if next(gen, stages) is stages:
                stages.remove(gen)


def _mixers(p_conv, p_hg, p_rw, conv_w, conv_b, lb_logits, hg_norm, mu, w0, w2, a0, a2, g2, k_k, k_a,
            r_k, ln_w, ln_b, layer, bsz, seqlen, n_chunks=MIX_BLOCK_CHUNKS):
    tb = n_chunks * CHUNK
    nb = seqlen // tb
    vec = lambda t: t.reshape(1, -1)
    row = lambda n: pl.BlockSpec((tb, n), lambda bi, j: (bi * nb + j, 0))
    conv_args = [p_conv, conv_w, vec(conv_b)]
    hgrn_args = [p_hg, lb_logits, vec(hg_norm), jnp.asarray(_hgrn_select_matrix(), BF16)]
    rwkv_args = [p_rw, vec(mu), vec(w0), w2.astype(BF16), vec(a0), a2.astype(BF16), g2.astype(BF16),
                 vec(k_k), vec(k_a), vec(r_k), vec(ln_w), vec(ln_b),
                 jnp.asarray(_block_select_matrix(n_chunks), BF16)]
    assert (len(conv_args), len(hgrn_args), len(rwkv_args)) == (N_CONV_IN, N_HGRN_IN, N_RWKV_IN)
    args = conv_args + hgrn_args + rwkv_args + [jnp.asarray(_head_group_matrix(HG_W), BF16)]
    streamed = {0: CONV_PROJ, N_CONV_IN: HG_PROJ, N_CONV_IN + N_HGRN_IN: RW_PROJ}
    in_specs = [row(streamed[i]) if i in streamed else _const_spec(t.shape) for i, t in enumerate(args)]
    mix_w = CONV_W + HG_W + RW_W
    return pl.pallas_call(
        functools.partial(_mixers_kernel, layer=layer, n_chunks=n_chunks),
        grid=(bsz, nb),
        in_specs=in_specs,
        out_specs=row(mix_w),
        out_shape=jax.ShapeDtypeStruct((bsz * seqlen, mix_w), BF16),
        scratch_shapes=[pltpu.VMEM((8, CONV_W), F32), pltpu.VMEM((HG_PAIRS, PW, PW), F32),
                        pltpu.VMEM((RW_PAIRS, PW, PW), F32), pltpu.VMEM((8, RW_PROJ), F32)],
        compiler_params=pltpu.CompilerParams(dimension_semantics=("arbitrary", "arbitrary"),
                                             vmem_limit_bytes=VMEM_LIMIT_BYTES),
        name="mixers",
    )(*args)


def _memkv_kernel(m_ref, g_ref, w_ref, k_ref, v_ref):
    mn = _rms(m_ref[...], g_ref[...]).astype(BF16)
    kv = jnp.dot(mn, w_ref[...], preferred_element_type=F32)
    k_ref[...] = kv[:, :D_MODEL].astype(BF16)
    v_ref[...] = kv[:, D_MODEL:].astype(BF16)


def _mem_kv(mem2d, g, wkv, layer, mem_len):
    t = mem2d.shape[0]
    row = pl.BlockSpec((mem_len, D_MODEL), lambda i: (i, 0))
    return pl.pallas_call(
        _memkv_kernel,
        grid=(t // mem_len,),
        in_specs=[row, _const_spec((1, D_MODEL)), _layer_spec((D_MODEL, 2 * D_MODEL), layer)],
        out_specs=[row, row],
        out_shape=[jax.ShapeDtypeStruct((t, D_MODEL), BF16)] * 2,
        compiler_params=pltpu.CompilerParams(dimension_semantics=("arbitrary",),
                                             vmem_limit_bytes=VMEM_LIMIT_BYTES),
        name="mem_kv",
    )(mem2d, g.reshape(1, D_MODEL), wkv)


def _xattn_kernel(h_ref, y_ref, wmo_ref, g_ref, wq_ref, mk_ref, mv_ref, wo_ref, o_ref):
    h = h_ref[...] + jnp.dot(y_ref[...], wmo_ref[...], preferred_element_type=F32)
    hn = _rms(h, g_ref[...]).astype(BF16)
    q = jnp.dot(hn, wq_ref[...], preferred_element_type=F32).astype(BF16)
    scale = XA_HEAD_DIM ** -0.5
    outs = []
    for hd in range(XA_HEADS):
        sl = slice(hd * XA_HEAD_DIM, (hd + 1) * XA_HEAD_DIM)
        s = lax.dot_general(q[:, sl], mk_ref[:, sl], NT, preferred_element_type=F32) * scale
        pe = jnp.exp(s - jnp.max(s, axis=-1, keepdims=True))
        pr = pe / jnp.sum(pe, axis=-1, keepdims=True)
        outs.append(jnp.dot(pr.astype(BF16), mv_ref[:, sl], preferred_element_type=F32).astype(BF16))
    o = jnp.concatenate(outs, axis=-1)
    o_ref[...] = h + jnp.dot(o, wo_ref[...], preferred_element_type=F32)


def _mixout_xattn(h, y, w_mix_out, g, wq, mk, mv, wo, layer, bsz, seqlen, mem_len, tq=512):
    nq = seqlen // tq
    row = lambda n: pl.BlockSpec((tq, n), lambda bi, j: (bi * nq + j, 0))
    memspec = pl.BlockSpec((mem_len, D_MODEL), lambda bi, j: (bi, 0))
    wspec = _layer_spec((D_MODEL, D_MODEL), layer)
    return pl.pallas_call(
        _xattn_kernel,
        grid=(bsz, nq),
        in_specs=[row(D_MODEL), row(D_MODEL),
                  wspec, _const_spec((1, D_MODEL)), wspec, memspec, memspec, wspec],
        out_specs=row(D_MODEL),
        out_shape=jax.ShapeDtypeStruct((bsz * seqlen, D_MODEL), F32),
        compiler_params=pltpu.CompilerParams(dimension_semantics=("arbitrary", "arbitrary"),
                                             vmem_limit_bytes=VMEM_LIMIT_BYTES),
        name="mixout_xattn",
    )(h, y, w_mix_out, g.reshape(1, D_MODEL), wq, mk, mv, wo)


def kernel(x, mem, ffn1_norm, ffn1_w_in, ffn1_w_out, mix_norm, w_mix_in, w_mix_out, conv_w, conv_b, hgrn_lb_logits, hgrn_norm, rwkv_mu, rwkv_w0, rwkv_w2, rwkv_a0, rwkv_a2, rwkv_g2, rwkv_k_k, rwkv_k_a, rwkv_r_k, rwkv_ln_w, rwkv_ln_b, xattn_norm, mem_norm, xattn_wq, xattn_wkv, xattn_wo, ffn2_norm, ffn2_w_in, ffn2_w_out, final_norm):
    bsz, seqlen, _ = x.shape
    mem_len = mem.shape[1]
    depth = ffn1_norm.shape[0]
    h = x.reshape(bsz * seqlen, D_MODEL)
    mem2d = mem.reshape(bsz * mem_len, D_MODEL)
    (ffn1_w_in, ffn1_w_out, ffn2_w_in, ffn2_w_out, w_mix_in, w_mix_out, xattn_wq, xattn_wkv,
     xattn_wo) = (w.astype(BF16) for w in (ffn1_w_in, ffn1_w_out, ffn2_w_in, ffn2_w_out, w_mix_in,
                                           w_mix_out, xattn_wq, xattn_wkv, xattn_wo))
    for l in range(depth):
        h = _ffn(h, ffn1_norm[l], ffn1_w_in, ffn1_w_out, l)
        p_conv, p_hg, p_rw = _mix_in(h, mix_norm[l], w_mix_in, l)
        y = _mixers(p_conv, p_hg, p_rw, conv_w[l], conv_b[l], hgrn_lb_logits, hgrn_norm[l],
                    rwkv_mu[l], rwkv_w0[l], rwkv_w2[l], rwkv_a0[l], rwkv_a2[l], rwkv_g2[l],
                    rwkv_k_k[l], rwkv_k_a[l], rwkv_r_k[l], rwkv_ln_w[l], rwkv_ln_b[l], l, bsz, seqlen)
        mk, mv = _mem_kv(mem2d, mem_norm[l], xattn_wkv, l, mem_len)
        h = _mixout_xattn(h, y, w_mix_out, xattn_norm[l], xattn_wq,
                          mk, mv, xattn_wo, l, bsz, seqlen, mem_len)
        h = _ffn(h, ffn2_norm[l], ffn2_w_in, ffn2_w_out, l,
                 final_g=final_norm if l == depth - 1 else None)
    return h.reshape(bsz, seqlen, D_MODEL)
```

```python
import functools

import numpy as np
import jax
import jax.numpy as jnp
from jax import lax
from jax.experimental import pallas as pl
from jax.experimental.pallas import tpu as pltpu

F32 = jnp.float32
BF16 = jnp.bfloat16

D_MODEL = 1024
DEPTH = 4
D_FF = 2816
FFN_RES = 0.5
EPS = 1e-6
CONV_W = 256
HD = 64
PW = 2 * HD
HG_W = 384
HG_PAIRS = HG_W // PW
HG_EXP_CLIP = 80.0
RW_W = 384
RW_PAIRS = RW_W // PW
RW_DECAY_RANK = 64
RW_A_RANK = 64
RW_G_RANK = 128
RW_PROJ = 3 * RW_W + RW_DECAY_RANK + RW_A_RANK + RW_G_RANK
RW_DECAY_SCALE = 0.606531
RW_GN_EPS = 64e-5
CONV_PROJ = 3 * CONV_W
HG_PROJ = 4 * HG_W
IN_W = CONV_PROJ + HG_PROJ + RW_PROJ
CHUNK = 64
N_LEVELS = 6
MIX_BLOCK_CHUNKS = 4
XA_HEADS = 4
XA_HEAD_DIM = D_MODEL // XA_HEADS
XATTN_SUBTILES = 2

VMEM_LIMIT_BYTES = 56 * 1024 * 1024
MXU_TILE = 256
FFN_SPLITS = (0, 6 * MXU_TILE, D_FF)
assert all(b % MXU_TILE == 0 for b in FFN_SPLITS)

NN = (((1,), (0,)), ((), ()))
NT = (((1,), (1,)), ((), ()))
TN = (((0,), (0,)), ((), ()))


def _dot(a, b, dims=NN):
    return lax.dot_general(a.astype(BF16), b.astype(BF16), dims, preferred_element_type=F32)


def _dot_sel2(e, x):
    hi = x.astype(BF16)
    lo = (x - hi.astype(F32)).astype(BF16)
    d = lambda t: lax.dot_general(e, t, NN, preferred_element_type=F32)
    return d(hi) + d(lo)


def _rms(x, g):
    return x * lax.rsqrt(jnp.mean(x * x, axis=-1, keepdims=True) + EPS) * g


def _sigmoid(x):
    return jax.nn.sigmoid(x)


def _ffn_kernel(x_ref, g_ref, win_ref, wout_ref, *rest, final):
    if final:
        fg_ref, o_ref = rest
    else:
        (o_ref,) = rest
    x = x_ref[...]
    xn = _rms(x, g_ref[...]).astype(BF16)
    acc = jnp.zeros_like(x)
    for lo, hi in zip(FFN_SPLITS[:-1], FFN_SPLITS[1:]):
        gate = jnp.dot(xn, win_ref[:, lo:hi], preferred_element_type=F32)
        up = jnp.dot(xn, win_ref[:, D_FF + lo:D_FF + hi], preferred_element_type=F32)
        act = (gate * _sigmoid(gate) * up).astype(BF16)
        acc = acc + jnp.dot(act, wout_ref[lo:hi, :], preferred_element_type=F32)
    out = x + FFN_RES * acc
    if final:
        out = _rms(out, fg_ref[...])
    o_ref[...] = out


def _const_spec(shape):
    nd = len(shape)
    return pl.BlockSpec(shape, lambda *_: (0,) * nd)


def _layer_spec(shape, layer):
    return pl.BlockSpec((None,) + tuple(shape), lambda *_: (layer, 0, 0))


def _ffn(h, g, w_in, w_out, layer, final_g=None, tm=512):
    t = h.shape[0]
    final = final_g is not None
    in_specs = [pl.BlockSpec((tm, D_MODEL), lambda i: (i, 0)),
                _const_spec((1, D_MODEL)),
                _layer_spec((D_MODEL, 2 * D_FF), layer),
                _layer_spec((D_FF, D_MODEL), layer)]
    args = [h, g.reshape(1, D_MODEL), w_in, w_out]
    if final:
        in_specs.append(_const_spec((1, D_MODEL)))
        args.append(final_g.reshape(1, D_MODEL))
    return pl.pallas_call(
        functools.partial(_ffn_kernel, final=final),
        grid=(t // tm,),
        in_specs=in_specs,
        out_specs=pl.BlockSpec((tm, D_MODEL), lambda i: (i, 0)),
        out_shape=jax.ShapeDtypeStruct((t, D_MODEL), F32),
        compiler_params=pltpu.CompilerParams(dimension_semantics=("arbitrary",),
                                             vmem_limit_bytes=VMEM_LIMIT_BYTES),
        name="ffn_final" if final else "ffn",
    )(*args)


def _mixin_kernel(x_ref, g_ref, w_ref, pc_ref, ph_ref, pr_ref):
    xn = _rms(x_ref[...], g_ref[...]).astype(BF16)
    p = jnp.dot(xn, w_ref[...], preferred_element_type=F32)
    pc_ref[...] = p[:, :CONV_PROJ]
    ph_ref[...] = p[:, CONV_PROJ:CONV_PROJ + HG_PROJ]
    pr_ref[...] = p[:, CONV_PROJ + HG_PROJ:]


def _mix_in(h, g, w, layer, tm=512):
    t = h.shape[0]
    row = lambda n: pl.BlockSpec((tm, n), lambda i: (i, 0))
    return pl.pallas_call(
        _mixin_kernel,
        grid=(t // tm,),
        in_specs=[row(D_MODEL), _const_spec((1, D_MODEL)), _layer_spec((D_MODEL, IN_W), layer)],
        out_specs=[row(CONV_PROJ), row(HG_PROJ), row(RW_PROJ)],
        out_shape=[jax.ShapeDtypeStruct((t, CONV_PROJ), F32),
                   jax.ShapeDtypeStruct((t, HG_PROJ), F32),
                   jax.ShapeDtypeStruct((t, RW_PROJ), F32)],
        compiler_params=pltpu.CompilerParams(dimension_semantics=("arbitrary",),
                                             vmem_limit_bytes=VMEM_LIMIT_BYTES),
        name="mix_in",
    )(h, g.reshape(1, D_MODEL), w)


def _conv_stages(p_ref, w_ref, b_ref, y_ref, carry_ref):
    halves = 2
    tb = p_ref.shape[0]
    hb = tb // halves
    for i in range(halves):
        p = p_ref[i * hb:(i + 1) * hb, :]
        b_gate = p[:, :CONV_W]
        z = p[:, CONV_W:2 * CONV_W] * p[:, 2 * CONV_W:]
        row = lax.broadcasted_iota(jnp.int32, z.shape, 0)
        c0 = carry_ref[0:1, :]
        c1 = carry_ref[1:2, :]
        z1 = jnp.where(row == 0, c1, pltpu.roll(z, 1, 0))
        z2 = jnp.where(row == 0, c0, jnp.where(row == 1, c1, pltpu.roll(z, 2, 0)))
        w = w_ref[...]
        zc = w[0:1, :] * z2 + w[1:2, :] * z1 + w[2:3, :] * z
        y_ref[i * hb:(i + 1) * hb, 0:CONV_W] = (b_gate * (zc + b_ref[...])).astype(y_ref.dtype)
        carry_ref[0:2, :] = z[hb - 2:hb, :]
        yield


def _hgrn_select_matrix():
    c = CHUNK
    e = np.zeros((1 + N_LEVELS, c, c), np.float32)
    for i in range(c):
        e[0, i, :i + 1] = 1.0
        for l in range(N_LEVELS):
            half = 1 << l
            m = (i // (2 * half)) * 2 * half + half - 1
            if i > m:
                e[1 + l, i, m + 1:i + 1] = 1.0
            else:
                e[1 + l, i, i + 1:m + 1] = 1.0
    return e.reshape((1 + N_LEVELS) * c, c)


def _head_group_matrix(width):
    idx = np.arange(width) // HD
    return (idx[:, None] == idx[None, :]).astype(np.float32)


def _block_select_matrix(n_chunks):
    i = np.arange(n_chunks * CHUNK)
    same = (i[:, None] // CHUNK) == (i[None, :] // CHUNK)
    return (same & (i[None, :] <= i[:, None])).astype(np.float32)


def _pair_masks():
    lane_a = lax.broadcasted_iota(jnp.int32, (CHUNK, PW), 1) < HD
    bi = lax.broadcasted_iota(jnp.int32, (PW, PW), 0) < HD
    bj = lax.broadcasted_iota(jnp.int32, (PW, PW), 1) < HD
    return lane_a, bi == bj


def _hgrn_stages(p_ref, lbl_ref, ng_ref, e_ref, gm_ref, y_ref, st_ref, layer, n_chunks):
    p = p_ref[...]
    q = p[:, :HG_W]
    z = p[:, HG_W:2 * HG_W]
    v = p[:, 2 * HG_W:3 * HG_W]
    gt = p[:, 3 * HG_W:]

    lg = lbl_ref[...]
    ex = jnp.exp(lg - jnp.max(lg, axis=0, keepdims=True))
    sm = ex / jnp.sum(ex, axis=0, keepdims=True)
    lb = jnp.maximum(jnp.sum(sm[:layer + 1], axis=0, keepdims=True) - sm[0:1], 0.0)

    log_sig = jnp.minimum(z, 0.0) - jnp.log(1.0 + jnp.exp(-jnp.abs(z)))
    lf = log_sig + jnp.log(1.0 + lb * jnp.exp(jnp.minimum(-z, HG_EXP_CLIP)))
    k = (1.0 - lb) * _sigmoid(-z)
    qs = q * _sigmoid(q)
    yield

    lane_a, blockdiag = _pair_masks()
    rowi = lax.broadcasted_iota(jnp.int32, (CHUNK, 1), 0)
    ti = lax.broadcasted_iota(jnp.int32, (CHUNK, PW), 0)
    si = lax.broadcasted_iota(jnp.int32, (CHUNK, PW), 1) & (CHUNK - 1)
    xr = ti ^ si
    lower = ti > si
    diag = ti == si
    picks = [lower & (xr >= (1 << l)) & (xr < (2 << l)) for l in range(N_LEVELS)]
    zero_b = jnp.zeros((CHUNK, PW), BF16)
    stack = lambda t: jnp.concatenate([jnp.where(lane_a, t, zero_b), jnp.where(lane_a, zero_b, t)], axis=0)
    k_b, v_b = k.astype(BF16), v.astype(BF16)

    e_mat = e_ref[...]
    inst = [(c, pr) for c in range(n_chunks) for pr in range(HG_PAIRS)]
    rows = lambda c: slice(c * CHUNK, (c + 1) * CHUNK)
    lanes = lambda pr: slice(pr * PW, (pr + 1) * PW)
    sums = [jnp.minimum(_dot_sel2(e_mat, lf[rows(c)]), 0.0) for c in range(n_chunks)]
    yield
    qb, kd, e_last, ql, kl = [], [], [], [], []
    for c in range(n_chunks):
        s_c, qs_c, k_c = sums[c], qs[rows(c)], k[rows(c)]
        b = s_c[0:CHUNK]
        b_last = b[CHUNK - 1:CHUNK]
        qb.append(qs_c * jnp.exp(b))
        kd.append(k_c * jnp.exp(jnp.minimum(b_last - b, 0.0)))
        e_last.append(jnp.exp(b_last))
        ql_c, kl_c = [], []
        for l in range(N_LEVELS):
            w = jnp.exp(s_c[(1 + l) * CHUNK:(2 + l) * CHUNK])
            right = ((rowi >> l) & 1) == 1
            ql_c.append(jnp.where(right, qs_c * w, 0.0).astype(BF16))
            kl_c.append(jnp.where(right, 0.0, k_c * w).astype(BF16))
        ql.append(ql_c)
        kl.append(kl_c)
        yield
    att = {}
    for c, pr in inst:
        att[c, pr] = jnp.where(diag, _dot(qs[rows(c), lanes(pr)], stack(k_b[rows(c), lanes(pr)]), NT), 0.0)
    yield
    for l in range(N_LEVELS):
        for c, pr in inst:
            att[c, pr] = jnp.where(picks[l], _dot(ql[c][l][:, lanes(pr)], stack(kl[c][l][:, lanes(pr)]), NT),
                                   att[c, pr])
        yield
    kv = {(c, pr): jnp.where(blockdiag, _dot(v_b[rows(c), lanes(pr)], kd[c][:, lanes(pr)], TN), 0.0)
          for c, pr in inst}
    yield
    o_intra = {(c, pr): _dot(att[c, pr], stack(v_b[rows(c), lanes(pr)])) for c, pr in inst}
    yield
    st_in = {}
    for pr in range(HG_PAIRS):
        st = st_ref[pr]
        for c in range(n_chunks):
            st_in[c, pr] = st
            st = st * e_last[c][:, lanes(pr)] + kv[c, pr]
        st_ref[pr] = st
    o = jnp.concatenate(
        [jnp.concatenate([_dot(qb[c][:, lanes(pr)], st_in[c, pr], NT) + o_intra[c, pr]
                          for pr in range(HG_PAIRS)], axis=-1) for c in range(n_chunks)], axis=0)
    yield
    ms = _dot(o * o, gm_ref[...]) * (1.0 / HD)
    o = o * lax.rsqrt(ms + EPS)
    y_ref[:, CONV_W:CONV_W + HG_W] = (o * ng_ref[...] * (gt * _sigmoid(gt))).astype(y_ref.dtype)
    yield


def _rwkv_stages(p_ref, mu_ref, w0_ref, w2_ref, a0_ref, a2_ref, g2_ref, kk_ref, ka_ref, rk_ref,
                 lnw_ref, lnb_ref, sel_ref, gm_ref, y_ref, st_ref, prev_ref, n_chunks):
    tb = n_chunks * CHUNK
    p = p_ref[...]
    rowp = lax.broadcasted_iota(jnp.int32, p.shape, 0)
    p_prev = jnp.where(rowp == 0, prev_ref[0:1, :], pltpu.roll(p, 1, 0))
    prev_ref[0:1, :] = p[tb - 1:tb, :]
    pf = p + (p_prev - p) * mu_ref[...]

    r = pf[:, :RW_W]
    k = pf[:, RW_W:2 * RW_W]
    v = pf[:, 2 * RW_W:3 * RW_W]
    o1 = 3 * RW_W
    wd = pf[:, o1:o1 + RW_DECAY_RANK]
    ad = pf[:, o1 + RW_DECAY_RANK:o1 + RW_DECAY_RANK + RW_A_RANK]
    gd = pf[:, o1 + RW_DECAY_RANK + RW_A_RANK:]

    lw = -RW_DECAY_SCALE * _sigmoid(w0_ref[...] + _dot(jnp.tanh(wd), w2_ref[...]))
    a = _sigmoid(a0_ref[...] + _dot(ad, a2_ref[...]))
    g = _dot(_sigmoid(gd), g2_ref[...])
    yield

    gm = gm_ref[...]
    kk = k * kk_ref[...]
    kk = kk / jnp.maximum(jnp.sqrt(_dot(kk * kk, gm)), 1e-12)
    k = k * (1.0 + (a - 1.0) * ka_ref[...])
    a_vec = -kk
    b_vec = kk * a
    yield

    cw = _dot_sel2(sel_ref[...], lw)
    e_in = jnp.exp(cw)
    e_out = jnp.exp(-cw)
    e_rest = jnp.exp(jnp.concatenate(
        [cw[(c + 1) * CHUNK - 1:(c + 1) * CHUNK] - cw[c * CHUNK:(c + 1) * CHUNK] for c in range(n_chunks)],
        axis=0))
    a_t = a_vec * jnp.exp(cw - lw)
    r_t = r * e_in
    b_c = b_vec * e_out
    k_c = k * e_out
    b_h = b_vec * e_rest
    k_h = k * e_rest
    yield

    lane_a, blockdiag = _pair_masks()
    trow = lax.broadcasted_iota(jnp.int32, (CHUNK, PW), 0)
    scol = lax.broadcasted_iota(jnp.int32, (CHUNK, PW), 1) & (HD - 1)
    strict2 = trow > scol
    incl2 = trow >= scol
    eye_hi = jnp.where((trow == scol) & jnp.logical_not(lane_a), 1.0, 0.0)
    zero_blk = jnp.zeros((CHUNK, PW), BF16)
    under = lambda t: jnp.concatenate([jnp.zeros_like(t), t], axis=0)
    over = lambda t: jnp.concatenate([t, jnp.zeros_like(t)], axis=0)

    inst = [(c, pr) for c in range(n_chunks) for pr in range(RW_PAIRS)]
    heads = [(c, pr, hh) for c, pr in inst for hh in range(2)]
    blk = lambda t, c, pr: t[c * CHUNK:(c + 1) * CHUNK, pr * PW:(pr + 1) * PW]
    v_b = v.astype(BF16)
    at_m, abk = {}, {}
    for c, pr in inst:
        at, rt = blk(a_t, c, pr), blk(r_t, c, pr)
        at_m[c, pr, 0], at_m[c, pr, 1] = jnp.where(lane_a, at, 0.0), jnp.where(lane_a, 0.0, at)
        bc, kc = blk(b_c, c, pr), blk(k_c, c, pr)
        rhs = jnp.concatenate([jnp.where(lane_a, bc, 0.0), jnp.where(lane_a, kc, 0.0),
                               jnp.where(lane_a, 0.0, bc), jnp.where(lane_a, 0.0, kc)], axis=0)
        abk[c, pr] = _dot(jnp.concatenate([at, rt], axis=0), rhs, NT)
    yield
    cmat, rbk, av = {}, {}, {}
    for c, pr, hh in heads:
        hl = slice(hh * PW, (hh + 1) * PW)
        nk = jnp.where(strict2, abk[c, pr][:CHUNK, hl], 0.0)
        rbk[c, pr, hh] = jnp.where(incl2, abk[c, pr][CHUNK:, hl], 0.0)
        av[c, pr, hh] = _dot(jnp.where(lane_a, 0.0, nk), under(blk(v_b, c, pr)))
        cmat[c, pr, hh] = jnp.where(lane_a, nk, eye_hi)
    yield
    for _ in range(N_LEVELS):
        cb = {i: cmat[i].astype(BF16) for i in heads}
        cmat = {i: _dot(cb[i], over(cb[i])) + jnp.where(lane_a, 0.0, cmat[i]) for i in heads}
        yield
    zs = {i: _dot(cmat[i], under(jnp.concatenate([at_m[i], av[i]], axis=1))) for i in heads}
    yield
    rz = {i: _dot(rbk[i], jnp.concatenate(
        [zs[i].astype(BF16), jnp.concatenate([zero_blk, blk(v_b, i[0], i[1])], axis=1)], axis=0)) for i in heads}
    yield
    pick = lambda d, c, pr, sl: jnp.where(lane_a, d[c, pr, 0][:, sl], d[c, pr, 1][:, sl])
    lo, hi = slice(0, PW), slice(PW, 2 * PW)
    rbar, ybar, gmat, delta = {}, {}, {}, {}
    for c, pr in inst:
        bh, kh, vv = blk(b_h, c, pr).astype(BF16), blk(k_h, c, pr).astype(BF16), blk(v_b, c, pr)
        rbar[c, pr] = blk(r_t, c, pr) + pick(rz, c, pr, lo)
        ybar[c, pr] = pick(rz, c, pr, hi)
        gmat[c, pr] = jnp.where(blockdiag, _dot(pick(zs, c, pr, lo), bh, TN), 0.0)
        delta[c, pr] = jnp.where(blockdiag, _dot(jnp.concatenate([pick(zs, c, pr, hi).astype(BF16), vv], axis=0),
                                                 jnp.concatenate([bh, kh], axis=0), TN), 0.0)
    yield
    s = [st_ref[pr] for pr in range(RW_PAIRS)]
    ys = {}
    for c in range(n_chunks):
        for pr in range(RW_PAIRS):
            ys[c, pr] = _dot(rbar[c, pr], s[pr], NT) + ybar[c, pr]
        s = [s[pr] * blk(e_in, c, pr)[CHUNK - 1:CHUNK] + _dot(s[pr], gmat[c, pr]) + delta[c, pr]
             for pr in range(RW_PAIRS)]
        yield
    for pr in range(RW_PAIRS):
        st_ref[pr] = s[pr]
    y = jnp.concatenate([jnp.concatenate([ys[c, pr] for pr in range(RW_PAIRS)], axis=-1)
                         for c in range(n_chunks)], axis=0)

    inv_hd = 1.0 / HD
    mean = _dot(y, gm) * inv_hd
    yc = y - mean
    var = _dot(yc * yc, gm) * inv_hd
    yn = yc * lax.rsqrt(var + RW_GN_EPS) * lnw_ref[...] + lnb_ref[...]
    bonus = _dot(r * k * rk_ref[...], gm) * v
    y_ref[:, CONV_W + HG_W:] = ((yn + bonus) * g).astype(y_ref.dtype)
    yield


N_CONV_IN, N_HGRN_IN, N_RWKV_IN = 3, 4, 13


def _mixers_kernel(*refs, layer, n_chunks):
    it = iter(refs)
    take = lambda n: [next(it) for _ in range(n)]
    conv_in, hgrn_in, rwkv_in = take(N_CONV_IN), take(N_HGRN_IN), take(N_RWKV_IN)
    gm_ref, y_ref, conv_carry, hg_state, rw_state, rw_prev = take(6)

    @pl.when(pl.program_id(1) == 0)
    def _():
        for ref in (conv_carry, hg_state, rw_state, rw_prev):
            ref[...] = jnp.zeros_like(ref)

    stages = [
        _rwkv_stages(*rwkv_in, gm_ref, y_ref, rw_state, rw_prev, n_chunks),
        _hgrn_stages(*hgrn_in, gm_ref, y_ref, hg_state, layer, n_chunks),
        _conv_stages(*conv_in, y_ref, conv_carry),
    ]
    while stages:
        for gen in list(stages):
            if next(gen, stages) is stages:
                stages.remove(gen)


def _mixers(p_conv, p_hg, p_rw, conv_w, conv_b, lb_logits, hg_norm, mu, w0, w2, a0, a2, g2, k_k, k_a,
            r_k, ln_w, ln_b, layer, bsz, seqlen, n_chunks=MIX_BLOCK_CHUNKS):
    tb = n_chunks * CHUNK
    nb = seqlen // tb
    vec = lambda t: t.reshape(1, -1)
    row = lambda n: pl.BlockSpec((tb, n), lambda bi, j: (bi * nb + j, 0))
    conv_args = [p_conv, conv_w, vec(conv_b)]
    hgrn_args = [p_hg, lb_logits, vec(hg_norm), jnp.asarray(_hgrn_select_matrix(), BF16)]
    rwkv_args = [p_rw, vec(mu), vec(w0), w2.astype(BF16), vec(a0), a2.astype(BF16), g2.astype(BF16),
                 vec(k_k), vec(k_a), vec(r_k), vec(ln_w), vec(ln_b),
                 jnp.asarray(_block_select_matrix(n_chunks), BF16)]
    assert (len(conv_args), len(hgrn_args), len(rwkv_args)) == (N_CONV_IN, N_HGRN_IN, N_RWKV_IN)
    args = conv_args + hgrn_args + rwkv_args + [jnp.asarray(_head_group_matrix(HG_W), BF16)]
    streamed = {0: CONV_PROJ, N_CONV_IN: HG_PROJ, N_CONV_IN + N_HGRN_IN: RW_PROJ}
    in_specs = [row(streamed[i]) if i in streamed else _const_spec(t.shape) for i, t in enumerate(args)]
    mix_w = CONV_W + HG_W + RW_W
    return pl.pallas_call(
        functools.partial(_mixers_kernel, layer=layer, n_chunks=n_chunks),
        grid=(bsz, nb),
        in_specs=in_specs,
        out_specs=row(mix_w),
        out_shape=jax.ShapeDtypeStruct((bsz * seqlen, mix_w), BF16),
        scratch_shapes=[pltpu.VMEM((8, CONV_W), F32), pltpu.VMEM((HG_PAIRS, PW, PW), F32),
                        pltpu.VMEM((RW_PAIRS, PW, PW), F32), pltpu.VMEM((8, RW_PROJ), F32)],
        compiler_params=pltpu.CompilerParams(dimension_semantics=("arbitrary", "arbitrary"),
                                             vmem_limit_bytes=VMEM_LIMIT_BYTES),
        name="mixers",
    )(*args)


def _memkv_kernel(m_ref, g_ref, w_ref, k_ref, v_ref):
    mn = _rms(m_ref[...], g_ref[...]).astype(BF16)
    kv = jnp.dot(mn, w_ref[...], preferred_element_type=F32)
    k_ref[...] = kv[:, :D_MODEL].astype(BF16)
    v_ref[...] = kv[:, D_MODEL:].astype(BF16)


def _mem_kv(mem2d, g, wkv, layer, mem_len):
    t = mem2d.shape[0]
    row = pl.BlockSpec((mem_len, D_MODEL), lambda i: (i, 0))
    return pl.pallas_call(
        _memkv_kernel,
        grid=(t // mem_len,),
        in_specs=[row, _const_spec((1, D_MODEL)), _layer_spec((D_MODEL, 2 * D_MODEL), layer)],
        out_specs=[row, row],
        out_shape=[jax.ShapeDtypeStruct((t, D_MODEL), BF16)] * 2,
        compiler_params=pltpu.CompilerParams(dimension_semantics=("arbitrary",),
                                             vmem_limit_bytes=VMEM_LIMIT_BYTES),
        name="mem_kv",
    )(mem2d, g.reshape(1, D_MODEL), wkv)


def _xattn_kernel(h_ref, y_ref, wmo_ref, g_ref, wq_ref, mk_ref, mv_ref, wo_ref, o_ref):
    hm = h_ref.shape[0] // XATTN_SUBTILES
    halves = [slice(i * hm, (i + 1) * hm) for i in range(XATTN_SUBTILES)]
    heads = [slice(hd * XA_HEAD_DIM, (hd + 1) * XA_HEAD_DIM) for hd in range(XA_HEADS)]
    scale = XA_HEAD_DIM ** -0.5
    h = [h_ref[r, :] + jnp.dot(y_ref[r, :], wmo_ref[...], preferred_element_type=F32) for r in halves]
    q = [jnp.dot(_rms(t, g_ref[...]).astype(BF16), wq_ref[...], preferred_element_type=F32).astype(BF16)
         for t in h]
    scores = lambda i: [lax.dot_general(q[i][:, sl], mk_ref[:, sl], NT, preferred_element_type=F32) * scale
                        for sl in heads]

    def softmax(s_heads):
        out = []
        for s in s_heads:
            pe = jnp.exp(s - jnp.max(s, axis=-1, keepdims=True))
            out.append((pe / jnp.sum(pe, axis=-1, keepdims=True)).astype(BF16))
        return out

    attend = lambda pr: jnp.concatenate(
        [jnp.dot(p, mv_ref[:, sl], preferred_element_type=F32).astype(BF16) for p, sl in zip(pr, heads)], axis=-1)

    def project_out(i, o):
        o_ref[halves[i], :] = h[i] + jnp.dot(o, wo_ref[...], preferred_element_type=F32)

    s = [scores(0)]
    done = None
    for i in range(XATTN_SUBTILES):
        if i + 1 < XATTN_SUBTILES:
            s.append(scores(i + 1))
        p = softmax(s[i])
        if done is not None:
            project_out(*done)
        done = (i, attend(p))
    project_out(*done)


def _mixout_xattn(h, y, w_mix_out, g, wq, mk, mv, wo, layer, bsz, seqlen, mem_len, tq=512):
    nq = seqlen // tq
    row = lambda n: pl.BlockSpec((tq, n), lambda bi, j: (bi * nq + j, 0))
    memspec = pl.BlockSpec((mem_len, D_MODEL), lambda bi, j: (bi, 0))
    wspec = _layer_spec((D_MODEL, D_MODEL), layer)
    return pl.pallas_call(
        _xattn_kernel,
        grid=(bsz, nq),
        in_specs=[row(D_MODEL), row(D_MODEL),
                  wspec, _const_spec((1, D_MODEL)), wspec, memspec, memspec, wspec],
        out_specs=row(D_MODEL),
        out_shape=jax.ShapeDtypeStruct((bsz * seqlen, D_MODEL), F32),
        compiler_params=pltpu.CompilerParams(dimension_semantics=("arbitrary", "arbitrary"),
                                             vmem_limit_bytes=VMEM_LIMIT_BYTES),
        name="mixout_xattn",
    )(h, y, w_mix_out, g.reshape(1, D_MODEL), wq, mk, mv, wo)


def kernel(x, mem, ffn1_norm, ffn1_w_in, ffn1_w_out, mix_norm, w_mix_in, w_mix_out, conv_w, conv_b, hgrn_lb_logits, hgrn_norm, rwkv_mu, rwkv_w0, rwkv_w2, rwkv_a0, rwkv_a2, rwkv_g2, rwkv_k_k, rwkv_k_a, rwkv_r_k, rwkv_ln_w, rwkv_ln_b, xattn_norm, mem_norm, xattn_wq, xattn_wkv, xattn_wo, ffn2_norm, ffn2_w_in, ffn2_w_out, final_norm):
    bsz, seqlen, _ = x.shape
    mem_len = mem.shape[1]
    depth = ffn1_norm.shape[0]
    h = x.reshape(bsz * seqlen, D_MODEL)
    mem2d = mem.reshape(bsz * mem_len, D_MODEL)
    (ffn1_w_in, ffn1_w_out, ffn2_w_in, ffn2_w_out, w_mix_in, w_mix_out, xattn_wq, xattn_wkv,
     xattn_wo) = (w.astype(BF16) for w in (ffn1_w_in, ffn1_w_out, ffn2_w_in, ffn2_w_out, w_mix_in,
                                           w_mix_out, xattn_wq, xattn_wkv, xattn_wo))
    for l in range(depth):
        h = _ffn(h, ffn1_norm[l], ffn1_w_in, ffn1_w_out, l)
        p_conv, p_hg, p_rw = _mix_in(h, mix_norm[l], w_mix_in, l)
        y = _mixers(p_conv, p_hg, p_rw, conv_w[l], conv_b[l], hgrn_lb_logits, hgrn_norm[l],
                    rwkv_mu[l], rwkv_w0[l], rwkv_w2[l], rwkv_a0[l], rwkv_a2[l], rwkv_g2[l],
                    rwkv_k_k[l], rwkv_k_a[l], rwkv_r_k[l], rwkv_ln_w[l], rwkv_ln_b[l], l, bsz, seqlen)
        mk, mv = _mem_kv(mem2d, mem_norm[l], xattn_wkv, l, mem_len)
        h = _mixout_xattn(h, y, w_mix_out, xattn_norm[l], xattn_wq,
                          mk, mv, xattn_wo, l, bsz, seqlen, mem_len)
        h = _ffn(h, ffn2_norm[l], ffn2_w_in, ffn2_w_out, l,
                 final_g=final_norm if l == depth - 1 else None)
    return h.reshape(bsz, seqlen, D_MODEL)
```

```python
import functools

import numpy as np
import jax
import jax.numpy as jnp
from jax import lax
from jax.experimental import pallas as pl
from jax.experimental.pallas import tpu as pltpu

F32 = jnp.float32
BF16 = jnp.bfloat16

D_MODEL = 1024
DEPTH = 4
D_FF = 2816
FFN_RES = 0.5
EPS = 1e-6
CONV_W = 256
HD = 64
PW = 2 * HD
HG_W = 384
HG_PAIRS = HG_W // PW
HG_EXP_CLIP = 80.0
RW_W = 384
RW_PAIRS = RW_W // PW
RW_DECAY_RANK = 64
RW_A_RANK = 64
RW_G_RANK = 128
RW_PROJ = 3 * RW_W + RW_DECAY_RANK + RW_A_RANK + RW_G_RANK
RW_DECAY_SCALE = 0.606531
RW_GN_EPS = 64e-5
CONV_PROJ = 3 * CONV_W
HG_PROJ = 4 * HG_W
IN_W = CONV_PROJ + HG_PROJ + RW_PROJ
CHUNK = 64
N_LEVELS = 6
MIX_BLOCK_CHUNKS = 4
XA_HEADS = 4
XA_HEAD_DIM = D_MODEL // XA_HEADS
XATTN_SUBTILES = 2

VMEM_LIMIT_BYTES = 56 * 1024 * 1024
MXU_TILE = 256
FFN_SPLITS = (0, 6 * MXU_TILE, D_FF)
assert all(b % MXU_TILE == 0 for b in FFN_SPLITS)

NN = (((1,), (0,)), ((), ()))
NT = (((1,), (1,)), ((), ()))
TN = (((0,), (0,)), ((), ()))


def _dot(a, b, dims=NN):
    return lax.dot_general(a.astype(BF16), b.astype(BF16), dims, preferred_element_type=F32)


def _dot_sel2(e, x):
    hi = x.astype(BF16)
    lo = (x - hi.astype(F32)).astype(BF16)
    d = lambda t: lax.dot_general(e, t, NN, preferred_element_type=F32)
    return d(hi) + d(lo)


def _rms(x, g):
    return x * lax.rsqrt(jnp.mean(x * x, axis=-1, keepdims=True) + EPS) * g


def _sigmoid(x):
    return jax.nn.sigmoid(x)


def _ffn_kernel(x_ref, g_ref, win_ref, wout_ref, *rest, final):
    if final:
        fg_ref, o_ref = rest
    else:
        (o_ref,) = rest
    x = x_ref[...]
    xn = _rms(x, g_ref[...]).astype(BF16)
    acc = jnp.zeros_like(x)
    for lo, hi in zip(FFN_SPLITS[:-1], FFN_SPLITS[1:]):
        gate = jnp.dot(xn, win_ref[:, lo:hi], preferred_element_type=F32)
        up = jnp.dot(xn, win_ref[:, D_FF + lo:D_FF + hi], preferred_element_type=F32)
        act = (gate * _sigmoid(gate) * up).astype(BF16)
        acc = acc + jnp.dot(act, wout_ref[lo:hi, :], preferred_element_type=F32)
    out = x + FFN_RES * acc
    if final:
        out = _rms(out, fg_ref[...])
    o_ref[...] = out


def _const_spec(shape):
    nd = len(shape)
    return pl.BlockSpec(shape, lambda *_: (0,) * nd)


def _layer_spec(shape, layer):
    return pl.BlockSpec((None,) + tuple(shape), lambda *_: (layer, 0, 0))


def _ffn(h, g, w_in, w_out, layer, final_g=None, tm=512):
    t = h.shape[0]
    final = final_g is not None
    in_specs = [pl.BlockSpec((tm, D_MODEL), lambda i: (i, 0)),
                _const_spec((1, D_MODEL)),
                _layer_spec((D_MODEL, 2 * D_FF), layer),
                _layer_spec((D_FF, D_MODEL), layer)]
    args = [h, g.reshape(1, D_MODEL), w_in, w_out]
    if final:
        in_specs.append(_const_spec((1, D_MODEL)))
        args.append(final_g.reshape(1, D_MODEL))
    return pl.pallas_call(
        functools.partial(_ffn_kernel, final=final),
        grid=(t // tm,),
        in_specs=in_specs,
        out_specs=pl.BlockSpec((tm, D_MODEL), lambda i: (i, 0)),
        out_shape=jax.ShapeDtypeStruct((t, D_MODEL), F32),
        compiler_params=pltpu.CompilerParams(dimension_semantics=("arbitrary",),
                                             vmem_limit_bytes=VMEM_LIMIT_BYTES),
        name="ffn_final" if final else "ffn",
    )(*args)


def _mixin_kernel(x_ref, g_ref, w_ref, pc_ref, ph_ref, pr_ref):
    xn = _rms(x_ref[...], g_ref[...]).astype(BF16)
    p = jnp.dot(xn, w_ref[...], preferred_element_type=F32)
    pc_ref[...] = p[:, :CONV_PROJ]
    ph_ref[...] = p[:, CONV_PROJ:CONV_PROJ + HG_PROJ]
    pr_ref[...] = p[:, CONV_PROJ + HG_PROJ:]


def _mix_in(h, g, w, layer, tm=512):
    t = h.shape[0]
    row = lambda n: pl.BlockSpec((tm, n), lambda i: (i, 0))
    return pl.pallas_call(
        _mixin_kernel,
        grid=(t // tm,),
        in_specs=[row(D_MODEL), _const_spec((1, D_MODEL)), _layer_spec((D_MODEL, IN_W), layer)],
        out_specs=[row(CONV_PROJ), row(HG_PROJ), row(RW_PROJ)],
        out_shape=[jax.ShapeDtypeStruct((t, CONV_PROJ), F32),
                   jax.ShapeDtypeStruct((t, HG_PROJ), F32),
                   jax.ShapeDtypeStruct((t, RW_PROJ), F32)],
        compiler_params=pltpu.CompilerParams(dimension_semantics=("arbitrary",),
                                             vmem_limit_bytes=VMEM_LIMIT_BYTES),
        name="mix_in",
    )(h, g.reshape(1, D_MODEL), w)


def _conv_stages(p_ref, w_ref, b_ref, y_ref, carry_ref):
    halves = 2
    tb = p_ref.shape[0]
    hb = tb // halves
    for i in range(halves):
        p = p_ref[i * hb:(i + 1) * hb, :]
        b_gate = p[:, :CONV_W]
        z = p[:, CONV_W:2 * CONV_W] * p[:, 2 * CONV_W:]
        row = lax.broadcasted_iota(jnp.int32, z.shape, 0)
        c0 = carry_ref[0:1, :]
        c1 = carry_ref[1:2, :]
        z1 = jnp.where(row == 0, c1, pltpu.roll(z, 1, 0))
        z2 = jnp.where(row == 0, c0, jnp.where(row == 1, c1, pltpu.roll(z, 2, 0)))
        w = w_ref[...]
        zc = w[0:1, :] * z2 + w[1:2, :] * z1 + w[2:3, :] * z
        y_ref[i * hb:(i + 1) * hb, 0:CONV_W] = (b_gate * (zc + b_ref[...])).astype(y_ref.dtype)
        carry_ref[0:2, :] = z[hb - 2:hb, :]
        yield


def _hgrn_select_matrix():
    c = CHUNK
    e = np.zeros((1 + N_LEVELS, c, c), np.float32)
    for i in range(c):
        e[0, i, :i + 1] = 1.0
        for l in range(N_LEVELS):
            half = 1 << l
            m = (i // (2 * half)) * 2 * half + half - 1
            if i > m:
                e[1 + l, i, m + 1:i + 1] = 1.0
            else:
                e[1 + l, i, i + 1:m + 1] = 1.0
    return e.reshape((1 + N_LEVELS) * c, c)


def _head_group_matrix(width):
    idx = np.arange(width) // HD
    return (idx[:, None] == idx[None, :]).astype(np.float32)


def _block_select_matrix(n_chunks):
    i = np.arange(n_chunks * CHUNK)
    same = (i[:, None] // CHUNK) == (i[None, :] // CHUNK)
    return (same & (i[None, :] <= i[:, None])).astype(np.float32)


def _pair_masks():
    lane_a = lax.broadcasted_iota(jnp.int32, (CHUNK, PW), 1) < HD
    bi = lax.broadcasted_iota(jnp.int32, (PW, PW), 0) < HD
    bj = lax.broadcasted_iota(jnp.int32, (PW, PW), 1) < HD
    return lane_a, bi == bj


def _hgrn_stages(p_ref, lbl_ref, ng_ref, e_ref, gm_ref, y_ref, st_ref, layer, n_chunks):
    p = p_ref[...]
    q = p[:, :HG_W]
    z = p[:, HG_W:2 * HG_W]
    v = p[:, 2 * HG_W:3 * HG_W]
    gt = p[:, 3 * HG_W:]

    lg = lbl_ref[...]
    ex = jnp.exp(lg - jnp.max(lg, axis=0, keepdims=True))
    sm = ex / jnp.sum(ex, axis=0, keepdims=True)
    lb = jnp.maximum(jnp.sum(sm[:layer + 1], axis=0, keepdims=True) - sm[0:1], 0.0)

    log_sig = jnp.minimum(z, 0.0) - jnp.log(1.0 + jnp.exp(-jnp.abs(z)))
    lf = log_sig + jnp.log(1.0 + lb * jnp.exp(jnp.minimum(-z, HG_EXP_CLIP)))
    k = (1.0 - lb) * _sigmoid(-z)
    qs = q * _sigmoid(q)
    yield

    lane_a, blockdiag = _pair_masks()
    ti = lax.broadcasted_iota(jnp.int32, (CHUNK, PW), 0)
    si = lax.broadcasted_iota(jnp.int32, (CHUNK, PW), 1) & (CHUNK - 1)
    xr = ti ^ si
    lower = ti > si
    diag = ti == si
    picks = [lower & (xr >= (1 << l)) & (xr < (2 << l)) for l in range(N_LEVELS)]
    zero_b = jnp.zeros((CHUNK, PW), BF16)
    stack = lambda t: jnp.concatenate([jnp.where(lane_a, t, zero_b), jnp.where(lane_a, zero_b, t)], axis=0)
    k_b, v_b = k.astype(BF16), v.astype(BF16)

    e_mat = e_ref[...]
    inst = [(c, pr) for c in range(n_chunks) for pr in range(HG_PAIRS)]
    rows = lambda c: slice(c * CHUNK, (c + 1) * CHUNK)
    lanes = lambda pr: slice(pr * PW, (pr + 1) * PW)
    sums = [jnp.minimum(_dot_sel2(e_mat, lf[rows(c)]), 0.0) for c in range(n_chunks)]
    yield
    qb, kd, e_last, ql, kl = [], [], [], [], []
    for c in range(n_chunks):
        s_c, qs_c, k_c = sums[c], qs[rows(c)], k[rows(c)]
        b = s_c[0:CHUNK]
        b_last = b[CHUNK - 1:CHUNK]
        qb.append(qs_c * jnp.exp(b))
        kd.append(k_c * jnp.exp(b_last - b))
        e_last.append(jnp.exp(b_last))
        ql_c, kl_c = [], []
        for l in range(N_LEVELS):
            w = jnp.exp(s_c[(1 + l) * CHUNK:(2 + l) * CHUNK])
            ql_c.append((qs_c * w).astype(BF16))
            kl_c.append((k_c * w).astype(BF16))
        ql.append(ql_c)
        kl.append(kl_c)
        yield
    att = {}
    for c, pr in inst:
        att[c, pr] = jnp.where(diag, _dot(qs[rows(c), lanes(pr)], stack(k_b[rows(c), lanes(pr)]), NT), 0.0)
    yield
    for l in range(N_LEVELS):
        for c, pr in inst:
            att[c, pr] = jnp.where(picks[l], _dot(ql[c][l][:, lanes(pr)], stack(kl[c][l][:, lanes(pr)]), NT),
                                   att[c, pr])
        yield
    kv = {(c, pr): jnp.where(blockdiag, _dot(v_b[rows(c), lanes(pr)], kd[c][:, lanes(pr)], TN), 0.0)
          for c, pr in inst}
    yield
    o_intra = {(c, pr): _dot(att[c, pr], stack(v_b[rows(c), lanes(pr)])) for c, pr in inst}
    yield
    st_in = {}
    for pr in range(HG_PAIRS):
        st = st_ref[pr]
        for c in range(n_chunks):
            st_in[c, pr] = st
            st = st * e_last[c][:, lanes(pr)] + kv[c, pr]
        st_ref[pr] = st
    o = jnp.concatenate(
        [jnp.concatenate([_dot(qb[c][:, lanes(pr)], st_in[c, pr], NT) + o_intra[c, pr]
                          for pr in range(HG_PAIRS)], axis=-1) for c in range(n_chunks)], axis=0)
    yield
    ms = _dot(o * o, gm_ref[...]) * (1.0 / HD)
    o = o * lax.rsqrt(ms + EPS)
    y_ref[:, CONV_W:CONV_W + HG_W] = (o * ng_ref[...] * (gt * _sigmoid(gt))).astype(y_ref.dtype)
    yield


def _rwkv_stages(p_ref, mu_ref, w0_ref, w2_ref, a0_ref, a2_ref, g2_ref, kk_ref, ka_ref, rk_ref,
                 lnw_ref, lnb_ref, sel_ref, gm_ref, y_ref, st_ref, prev_ref, n_chunks):
    tb = n_chunks * CHUNK
    p = p_ref[...]
    rowp = lax.broadcasted_iota(jnp.int32, p.shape, 0)
    p_prev = jnp.where(rowp == 0, prev_ref[0:1, :], pltpu.roll(p, 1, 0))
    prev_ref[0:1, :] = p[tb - 1:tb, :]
    pf = p + (p_prev - p) * mu_ref[...]

    r = pf[:, :RW_W]
    k = pf[:, RW_W:2 * RW_W]
    v = pf[:, 2 * RW_W:3 * RW_W]
    o1 = 3 * RW_W
    wd = pf[:, o1:o1 + RW_DECAY_RANK]
    ad = pf[:, o1 + RW_DECAY_RANK:o1 + RW_DECAY_RANK + RW_A_RANK]
    gd = pf[:, o1 + RW_DECAY_RANK + RW_A_RANK:]

    lw = -RW_DECAY_SCALE * _sigmoid(w0_ref[...] + _dot(jnp.tanh(wd), w2_ref[...]))
    a = _sigmoid(a0_ref[...] + _dot(ad, a2_ref[...]))
    g = _dot(_sigmoid(gd), g2_ref[...])
    yield

    gm = gm_ref[...]
    kk = k * kk_ref[...]
    kk = kk / jnp.maximum(jnp.sqrt(_dot(kk * kk, gm)), 1e-12)
    k = k * (1.0 + (a - 1.0) * ka_ref[...])
    a_vec = -kk
    b_vec = kk * a
    yield

    cw = _dot_sel2(sel_ref[...], lw)
    e_in = jnp.exp(cw)
    e_out = jnp.exp(-cw)
    e_rest = jnp.exp(jnp.concatenate(
        [cw[(c + 1) * CHUNK - 1:(c + 1) * CHUNK] - cw[c * CHUNK:(c + 1) * CHUNK] for c in range(n_chunks)],
        axis=0))
    a_t = a_vec * jnp.exp(cw - lw)
    r_t = r * e_in
    b_c = b_vec * e_out
    k_c = k * e_out
    b_h = b_vec * e_rest
    k_h = k * e_rest
    yield

    lane_a, blockdiag = _pair_masks()
    trow = lax.broadcasted_iota(jnp.int32, (CHUNK, PW), 0)
    scol = lax.broadcasted_iota(jnp.int32, (CHUNK, PW), 1) & (HD - 1)
    strict2 = trow > scol
    incl2 = trow >= scol
    eye_hi = jnp.where((trow == scol) & jnp.logical_not(lane_a), 1.0, 0.0)
    zero_blk = jnp.zeros((CHUNK, PW), BF16)
    under = lambda t: jnp.concatenate([jnp.zeros_like(t), t], axis=0)
    over = lambda t: jnp.concatenate([t, jnp.zeros_like(t)], axis=0)

    inst = [(c, pr) for c in range(n_chunks) for pr in range(RW_PAIRS)]
    heads = [(c, pr, hh) for c, pr in inst for hh in range(2)]
    blk = lambda t, c, pr: t[c * CHUNK:(c + 1) * CHUNK, pr * PW:(pr + 1) * PW]
    v_b = v.astype(BF16)
    at_m, abk = {}, {}
    for c, pr in inst:
        at, rt = blk(a_t, c, pr), blk(r_t, c, pr)
        at_m[c, pr, 0], at_m[c, pr, 1] = jnp.where(lane_a, at, 0.0), jnp.where(lane_a, 0.0, at)
        bc, kc = blk(b_c, c, pr), blk(k_c, c, pr)
        rhs = jnp.concatenate([jnp.where(lane_a, bc, 0.0), jnp.where(lane_a, kc, 0.0),
                               jnp.where(lane_a, 0.0, bc), jnp.where(lane_a, 0.0, kc)], axis=0)
        abk[c, pr] = _dot(jnp.concatenate([at, rt], axis=0), rhs, NT)
    yield
    cmat, rbk, av = {}, {}, {}
    for c, pr, hh in heads:
        hl = slice(hh * PW, (hh + 1) * PW)
        nk = jnp.where(strict2, abk[c, pr][:CHUNK, hl], 0.0)
        rbk[c, pr, hh] = jnp.where(incl2, abk[c, pr][CHUNK:, hl], 0.0)
        av[c, pr, hh] = _dot(jnp.where(lane_a, 0.0, nk), under(blk(v_b, c, pr)))
        cmat[c, pr, hh] = jnp.where(lane_a, nk, eye_hi)
    yield
    for _ in range(N_LEVELS):
        cb = {i: cmat[i].astype(BF16) for i in heads}
        cmat = {i: _dot(cb[i], over(cb[i])) + jnp.where(lane_a, 0.0, cmat[i]) for i in heads}
        yield
    zs = {i: _dot(cmat[i], under(jnp.concatenate([at_m[i], av[i]], axis=1))) for i in heads}
    yield
    rz = {i: _dot(rbk[i], jnp.concatenate(
        [zs[i].astype(BF16), jnp.concatenate([zero_blk, blk(v_b, i[0], i[1])], axis=1)], axis=0)) for i in heads}
    yield
    pick = lambda d, c, pr, sl: jnp.where(lane_a, d[c, pr, 0][:, sl], d[c, pr, 1][:, sl])
    lo, hi = slice(0, PW), slice(PW, 2 * PW)
    rbar, ybar, gmat, delta = {}, {}, {}, {}
    for c, pr in inst:
        bh, kh, vv = blk(b_h, c, pr).astype(BF16), blk(k_h, c, pr).astype(BF16), blk(v_b, c, pr)
        rbar[c, pr] = blk(r_t, c, pr) + pick(rz, c, pr, lo)
        ybar[c, pr] = pick(rz, c, pr, hi)
        gmat[c, pr] = jnp.where(blockdiag, _dot(pick(zs, c, pr, lo), bh, TN), 0.0)
        delta[c, pr] = jnp.where(blockdiag, _dot(jnp.concatenate([pick(zs, c, pr, hi).astype(BF16), vv], axis=0),
                                                 jnp.concatenate([bh, kh], axis=0), TN), 0.0)
    yield
    s = [st_ref[pr] for pr in range(RW_PAIRS)]
    ys = {}
    for c in range(n_chunks):
        for pr in range(RW_PAIRS):
            ys[c, pr] = _dot(rbar[c, pr], s[pr], NT) + ybar[c, pr]
        s = [s[pr] * blk(e_in, c, pr)[CHUNK - 1:CHUNK] + _dot(s[pr], gmat[c, pr]) + delta[c, pr]
             for pr in range(RW_PAIRS)]
        yield
    for pr in range(RW_PAIRS):
        st_ref[pr] = s[pr]
    y = jnp.concatenate([jnp.concatenate([ys[c, pr] for pr in range(RW_PAIRS)], axis=-1)
                         for c in range(n_chunks)], axis=0)

    inv_hd = 1.0 / HD
    mean = _dot(y, gm) * inv_hd
    yc = y - mean
    var = _dot(yc * yc, gm) * inv_hd
    yn = yc * lax.rsqrt(var + RW_GN_EPS) * lnw_ref[...] + lnb_ref[...]
    bonus = _dot(r * k * rk_ref[...], gm) * v
    y_ref[:, CONV_W + HG_W:] = ((yn + bonus) * g).astype(y_ref.dtype)
    yield


N_CONV_IN, N_HGRN_IN, N_RWKV_IN = 3, 4, 13


def _mixers_kernel(*refs, layer, n_chunks):
    it = iter(refs)
    take = lambda n: [next(it) for _ in range(n)]
    conv_in, hgrn_in, rwkv_in = take(N_CONV_IN), take(N_HGRN_IN), take(N_RWKV_IN)
    gm_ref, y_ref, conv_carry, hg_state, rw_state, rw_prev = take(6)

    @pl.when(pl.program_id(1) == 0)
    def _():
        for ref in (conv_carry, hg_state, rw_state, rw_prev):
            ref[...] = jnp.zeros_like(ref)

    stages = [
        _rwkv_stages(*rwkv_in, gm_ref, y_ref, rw_state, rw_prev, n_chunks),
        _hgrn_stages(*hgrn_in, gm_ref, y_ref, hg_state, layer, n_chunks),
        _conv_stages(*conv_in, y_ref, conv_carry),
    ]
    while stages:
        for gen in list(stages):
            if next(gen, stages) is stages:
                stages.remove(gen)


def _mixers(p_conv, p_hg, p_rw, conv_w, conv_b, lb_logits, hg_norm, mu, w0, w2, a0, a2, g2, k_k, k_a,
            r_k, ln_w, ln_b, layer, bsz, seqlen, n_chunks=MIX_BLOCK_CHUNKS):
    tb = n_chunks * CHUNK
    nb = seqlen // tb
    vec = lambda t: t.reshape(1, -1)
    row = lambda n: pl.BlockSpec((tb, n), lambda bi, j: (bi * nb + j, 0))
    conv_args = [p_conv, conv_w, vec(conv_b)]
    hgrn_args = [p_hg, lb_logits, vec(hg_norm), jnp.asarray(_hgrn_select_matrix(), BF16)]
    rwkv_args = [p_rw, vec(mu), vec(w0), w2.astype(BF16), vec(a0), a2.astype(BF16), g2.astype(BF16),
                 vec(k_k), vec(k_a), vec(r_k), vec(ln_w), vec(ln_b),
                 jnp.asarray(_block_select_matrix(n_chunks), BF16)]
    assert (len(conv_args), len(hgrn_args), len(rwkv_args)) == (N_CONV_IN, N_HGRN_IN, N_RWKV_IN)
    args = conv_args + hgrn_args + rwkv_args + [jnp.asarray(_head_group_matrix(HG_W), BF16)]
    streamed = {0: CONV_PROJ, N_CONV_IN: HG_PROJ, N_CONV_IN + N_HGRN_IN: RW_PROJ}
    in_specs = [row(streamed[i]) if i in streamed else _const_spec(t.shape) for i, t in enumerate(args)]
    mix_w = CONV_W + HG_W + RW_W
    return pl.pallas_call(
        functools.partial(_mixers_kernel, layer=layer, n_chunks=n_chunks),
        grid=(bsz, nb),
        in_specs=in_specs,
        out_specs=row(mix_w),
        out_shape=jax.ShapeDtypeStruct((bsz * seqlen, mix_w), BF16),
        scratch_shapes=[pltpu.VMEM((8, CONV_W), F32), pltpu.VMEM((HG_PAIRS, PW, PW), F32),
                        pltpu.VMEM((RW_PAIRS, PW, PW), F32), pltpu.VMEM((8, RW_PROJ), F32)],
        compiler_params=pltpu.CompilerParams(dimension_semantics=("arbitrary", "arbitrary"),
                                             vmem_limit_bytes=VMEM_LIMIT_BYTES),
        name="mixers",
    )(*args)


def _memkv_kernel(m_ref, g_ref, w_ref, k_ref, v_ref):
    mn = _rms(m_ref[...], g_ref[...]).astype(BF16)
    kv = jnp.dot(mn, w_ref[...], preferred_element_type=F32)
    k_ref[...] = kv[:, :D_MODEL].astype(BF16)
    v_ref[...] = kv[:, D_MODEL:].astype(BF16)


def _mem_kv(mem2d, g, wkv, layer, mem_len):
    t = mem2d.shape[0]
    row = pl.BlockSpec((mem_len, D_MODEL), lambda i: (i, 0))
    return pl.pallas_call(
        _memkv_kernel,
        grid=(t // mem_len,),
        in_specs=[row, _const_spec((1, D_MODEL)), _layer_spec((D_MODEL, 2 * D_MODEL), layer)],
        out_specs=[row, row],
        out_shape=[jax.ShapeDtypeStruct((t, D_MODEL), BF16)] * 2,
        compiler_params=pltpu.CompilerParams(dimension_semantics=("arbitrary",),
                                             vmem_limit_bytes=VMEM_LIMIT_BYTES),
        name="mem_kv",
    )(mem2d, g.reshape(1, D_MODEL), wkv)


def _xattn_kernel(h_ref, y_ref, wmo_ref, g_ref, wq_ref, mk_ref, mv_ref, wo_ref, o_ref):
    hm = h_ref.shape[0] // XATTN_SUBTILES
    halves = [slice(i * hm, (i + 1) * hm) for i in range(XATTN_SUBTILES)]
    heads = [slice(hd * XA_HEAD_DIM, (hd + 1) * XA_HEAD_DIM) for hd in range(XA_HEADS)]
    scale = XA_HEAD_DIM ** -0.5
    h = [h_ref[r, :] + jnp.dot(y_ref[r, :], wmo_ref[...], preferred_element_type=F32) for r in halves]
    q = [jnp.dot(_rms(t, g_ref[...]).astype(BF16), wq_ref[...], preferred_element_type=F32).astype(BF16)
         for t in h]
    scores = lambda i: [lax.dot_general(q[i][:, sl], mk_ref[:, sl], NT, preferred_element_type=F32) * scale
                        for sl in heads]

    def softmax(s_heads):
        out = []
        for s in s_heads:
            pe = jnp.exp(s - jnp.max(s, axis=-1, keepdims=True))
            out.append((pe / jnp.sum(pe, axis=-1, keepdims=True)).astype(BF16))
        return out

    attend = lambda pr: jnp.concatenate(
        [jnp.dot(p, mv_ref[:, sl], preferred_element_type=F32).astype(BF16) for p, sl in zip(pr, heads)], axis=-1)

    def project_out(i, o):
        o_ref[halves[i], :] = h[i] + jnp.dot(o, wo_ref[...], preferred_element_type=F32)

    s = [scores(0)]
    done = None
    for i in range(XATTN_SUBTILES):
        if i + 1 < XATTN_SUBTILES:
            s.append(scores(i + 1))
        p = softmax(s[i])
        if done is not None:
            project_out(*done)
        done = (i, attend(p))
    project_out(*done)


def _mixout_xattn(h, y, w_mix_out, g, wq, mk, mv, wo, layer, bsz, seqlen, mem_len, tq=512):
    nq = seqlen // tq
    row = lambda n: pl.BlockSpec((tq, n), lambda bi, j: (bi * nq + j, 0))
    memspec = pl.BlockSpec((mem_len, D_MODEL), lambda bi, j: (bi, 0))
    wspec = _layer_spec((D_MODEL, D_MODEL), layer)
    return pl.pallas_call(
        _xattn_kernel,
        grid=(bsz, nq),
        in_specs=[row(D_MODEL), row(D_MODEL),
                  wspec, _const_spec((1, D_MODEL)), wspec, memspec, memspec, wspec],
        out_specs=row(D_MODEL),
        out_shape=jax.ShapeDtypeStruct((bsz * seqlen, D_MODEL), F32),
        compiler_params=pltpu.CompilerParams(dimension_semantics=("arbitrary", "arbitrary"),
                                             vmem_limit_bytes=VMEM_LIMIT_BYTES),
        name="mixout_xattn",
    )(h, y, w_mix_out, g.reshape(1, D_MODEL), wq, mk, mv, wo)


def kernel(x, mem, ffn1_norm, ffn1_w_in, ffn1_w_out, mix_norm, w_mix_in, w_mix_out, conv_w, conv_b, hgrn_lb_logits, hgrn_norm, rwkv_mu, rwkv_w0, rwkv_w2, rwkv_a0, rwkv_a2, rwkv_g2, rwkv_k_k, rwkv_k_a, rwkv_r_k, rwkv_ln_w, rwkv_ln_b, xattn_norm, mem_norm, xattn_wq, xattn_wkv, xattn_wo, ffn2_norm, ffn2_w_in, ffn2_w_out, final_norm):
    bsz, seqlen, _ = x.shape
    mem_len = mem.shape[1]
    depth = ffn1_norm.shape[0]
    h = x.reshape(bsz * seqlen, D_MODEL)
    mem2d = mem.reshape(bsz * mem_len, D_MODEL)
    (ffn1_w_in, ffn1_w_out, ffn2_w_in, ffn2_w_out, w_mix_in, w_mix_out, xattn_wq, xattn_wkv,
     xattn_wo) = (w.astype(BF16) for w in (ffn1_w_in, ffn1_w_out, ffn2_w_in, ffn2_w_out, w_mix_in,
                                           w_mix_out, xattn_wq, xattn_wkv, xattn_wo))
    for l in range(depth):
        h = _ffn(h, ffn1_norm[l], ffn1_w_in, ffn1_w_out, l)
        p_conv, p_hg, p_rw = _mix_in(h, mix_norm[l], w_mix_in, l)
        y = _mixers(p_conv, p_hg, p_rw, conv_w[l], conv_b[l], hgrn_lb_logits, hgrn_norm[l],
                    rwkv_mu[l], rwkv_w0[l], rwkv_w2[l], rwkv_a0[l], rwkv_a2[l], rwkv_g2[l],
                    rwkv_k_k[l], rwkv_k_a[l], rwkv_r_k[l], rwkv_ln_w[l], rwkv_ln_b[l], l, bsz, seqlen)
        mk, mv = _mem_kv(mem2d, mem_norm[l], xattn_wkv, l, mem_len)
        h = _mixout_xattn(h, y, w_mix_out, xattn_norm[l], xattn_wq,
                          mk, mv, xattn_wo, l, bsz, seqlen, mem_len)
        h = _ffn(h, ffn2_norm[l], ffn2_w_in, ffn2_w_out, l,
                 final_g=final_norm if l == depth - 1 else None)
    return h.reshape(bsz, seqlen, D_MODEL)
```

```python
import functools

import numpy as np
import jax
import jax.numpy as jnp
from jax import lax
from jax.experimental import pallas as pl
from jax.experimental.pallas import tpu as pltpu

F32 = jnp.float32
BF16 = jnp.bfloat16

D_MODEL = 1024
DEPTH = 4
D_FF = 2816
FFN_RES = 0.5
EPS = 1e-6
CONV_W = 256
HD = 64
PW = 2 * HD
HG_W = 384
HG_PAIRS = HG_W // PW
HG_EXP_CLIP = 80.0
RW_W = 384
RW_PAIRS = RW_W // PW
RW_DECAY_RANK = 64
RW_A_RANK = 64
RW_G_RANK = 128
RW_PROJ = 3 * RW_W + RW_DECAY_RANK + RW_A_RANK + RW_G_RANK
RW_DECAY_SCALE = 0.606531
LOG2_E = 1.4426950408889634
RW_GN_EPS = 64e-5
CONV_PROJ = 3 * CONV_W
HG_PROJ = 4 * HG_W
IN_W = CONV_PROJ + HG_PROJ + RW_PROJ
CHUNK = 64
N_LEVELS = 6
MIX_BLOCK_CHUNKS = 4
XA_HEADS = 4
XA_HEAD_DIM = D_MODEL // XA_HEADS
XATTN_SUBTILES = 2

VMEM_LIMIT_BYTES = 56 * 1024 * 1024
MXU_TILE = 256
FFN_SPLITS = (0, 6 * MXU_TILE, D_FF)
assert all(b % MXU_TILE == 0 for b in FFN_SPLITS)

NN = (((1,), (0,)), ((), ()))
NT = (((1,), (1,)), ((), ()))
TN = (((0,), (0,)), ((), ()))


def _dot(a, b, dims=NN):
    return lax.dot_general(a.astype(BF16), b.astype(BF16), dims, preferred_element_type=F32)


def _dot_sel2(e, x):
    hi = x.astype(BF16)
    lo = (x - hi.astype(F32)).astype(BF16)
    d = lambda t: lax.dot_general(e, t, NN, preferred_element_type=F32)
    return d(hi) + d(lo)


def _rms(x, g):
    return x * lax.rsqrt(jnp.mean(x * x, axis=-1, keepdims=True) + EPS) * g


def _sigmoid(x):
    return jax.nn.sigmoid(x)


def _ffn_kernel(x_ref, g_ref, win_ref, wout_ref, *rest, final):
    if final:
        fg_ref, o_ref = rest
    else:
        (o_ref,) = rest
    x = x_ref[...]
    xn = _rms(x, g_ref[...]).astype(BF16)
    acc = jnp.zeros_like(x)
    for lo, hi in zip(FFN_SPLITS[:-1], FFN_SPLITS[1:]):
        gate = jnp.dot(xn, win_ref[:, lo:hi], preferred_element_type=F32)
        up = jnp.dot(xn, win_ref[:, D_FF + lo:D_FF + hi], preferred_element_type=F32)
        act = (gate * _sigmoid(gate) * up).astype(BF16)
        acc = acc + jnp.dot(act, wout_ref[lo:hi, :], preferred_element_type=F32)
    out = x + FFN_RES * acc
    if final:
        out = _rms(out, fg_ref[...])
    o_ref[...] = out


def _const_spec(shape):
    nd = len(shape)
    return pl.BlockSpec(shape, lambda *_: (0,) * nd)


def _layer_spec(shape, layer):
    return pl.BlockSpec((None,) + tuple(shape), lambda *_: (layer, 0, 0))


def _ffn(h, g, w_in, w_out, layer, final_g=None, tm=512):
    t = h.shape[0]
    final = final_g is not None
    in_specs = [pl.BlockSpec((tm, D_MODEL), lambda i: (i, 0)),
                _const_spec((1, D_MODEL)),
                _layer_spec((D_MODEL, 2 * D_FF), layer),
                _layer_spec((D_FF, D_MODEL), layer)]
    args = [h, g.reshape(1, D_MODEL), w_in, w_out]
    if final:
        in_specs.append(_const_spec((1, D_MODEL)))
        args.append(final_g.reshape(1, D_MODEL))
    return pl.pallas_call(
        functools.partial(_ffn_kernel, final=final),
        grid=(t // tm,),
        in_specs=in_specs,
        out_specs=pl.BlockSpec((tm, D_MODEL), lambda i: (i, 0)),
        out_shape=jax.ShapeDtypeStruct((t, D_MODEL), F32),
        compiler_params=pltpu.CompilerParams(dimension_semantics=("arbitrary",),
                                             vmem_limit_bytes=VMEM_LIMIT_BYTES),
        name="ffn_final" if final else "ffn",
    )(*args)


def _mixin_kernel(x_ref, g_ref, w_ref, pc_ref, ph_ref, pr_ref):
    xn = _rms(x_ref[...], g_ref[...]).astype(BF16)
    p = jnp.dot(xn, w_ref[...], preferred_element_type=F32)
    pc_ref[...] = p[:, :CONV_PROJ]
    ph_ref[...] = p[:, CONV_PROJ:CONV_PROJ + HG_PROJ]
    pr_ref[...] = p[:, CONV_PROJ + HG_PROJ:]


def _mix_in(h, g, w, layer, tm=512):
    t = h.shape[0]
    row = lambda n: pl.BlockSpec((tm, n), lambda i: (i, 0))
    return pl.pallas_call(
        _mixin_kernel,
        grid=(t // tm,),
        in_specs=[row(D_MODEL), _const_spec((1, D_MODEL)), _layer_spec((D_MODEL, IN_W), layer)],
        out_specs=[row(CONV_PROJ), row(HG_PROJ), row(RW_PROJ)],
        out_shape=[jax.ShapeDtypeStruct((t, CONV_PROJ), F32),
                   jax.ShapeDtypeStruct((t, HG_PROJ), F32),
                   jax.ShapeDtypeStruct((t, RW_PROJ), F32)],
        compiler_params=pltpu.CompilerParams(dimension_semantics=("arbitrary",),
                                             vmem_limit_bytes=VMEM_LIMIT_BYTES),
        name="mix_in",
    )(h, g.reshape(1, D_MODEL), w)


def _conv_stages(p_ref, w_ref, b_ref, y_ref, carry_ref):
    halves = 2
    tb = p_ref.shape[0]
    hb = tb // halves
    for i in range(halves):
        p = p_ref[i * hb:(i + 1) * hb, :]
        b_gate = p[:, :CONV_W]
        z = p[:, CONV_W:2 * CONV_W] * p[:, 2 * CONV_W:]
        row = lax.broadcasted_iota(jnp.int32, z.shape, 0)
        c0 = carry_ref[0:1, :]
        c1 = carry_ref[1:2, :]
        z1 = jnp.where(row == 0, c1, pltpu.roll(z, 1, 0))
        z2 = jnp.where(row == 0, c0, jnp.where(row == 1, c1, pltpu.roll(z, 2, 0)))
        w = w_ref[...]
        zc = w[0:1, :] * z2 + w[1:2, :] * z1 + w[2:3, :] * z
        y_ref[i * hb:(i + 1) * hb, 0:CONV_W] = (b_gate * (zc + b_ref[...])).astype(y_ref.dtype)
        carry_ref[0:2, :] = z[hb - 2:hb, :]
        yield


def _hgrn_select_matrix():
    c = CHUNK
    e = np.zeros((1 + N_LEVELS, c, c), np.float32)
    for i in range(c):
        e[0, i, :i + 1] = 1.0
        for l in range(N_LEVELS):
            half = 1 << l
            m = (i // (2 * half)) * 2 * half + half - 1
            if i > m:
                e[1 + l, i, m + 1:i + 1] = 1.0
            else:
                e[1 + l, i, i + 1:m + 1] = 1.0
    return e.reshape((1 + N_LEVELS) * c, c)


def _head_group_matrix(width):
    idx = np.arange(width) // HD
    return (idx[:, None] == idx[None, :]).astype(np.float32)


def _block_select_matrix(n_chunks):
    i = np.arange(n_chunks * CHUNK)
    same = (i[:, None] // CHUNK) == (i[None, :] // CHUNK)
    return (same & (i[None, :] <= i[:, None])).astype(np.float32)


def _pair_masks():
    lane_a = lax.broadcasted_iota(jnp.int32, (CHUNK, PW), 1) < HD
    bi = lax.broadcasted_iota(jnp.int32, (PW, PW), 0) < HD
    bj = lax.broadcasted_iota(jnp.int32, (PW, PW), 1) < HD
    return lane_a, bi == bj


def _hgrn_stages(p_ref, lbl_ref, ng_ref, e_ref, gm_ref, y_ref, st_ref, layer, n_chunks):
    p = p_ref[...]
    q = p[:, :HG_W]
    z = p[:, HG_W:2 * HG_W]
    v = p[:, 2 * HG_W:3 * HG_W]
    gt = p[:, 3 * HG_W:]

    lg = lbl_ref[...]
    ex = jnp.exp(lg - jnp.max(lg, axis=0, keepdims=True))
    sm = ex / jnp.sum(ex, axis=0, keepdims=True)
    lb = jnp.maximum(jnp.sum(sm[:layer + 1], axis=0, keepdims=True) - sm[0:1], 0.0)

    e_abs = jnp.exp(-jnp.abs(z))
    den = 1.0 + e_abs
    log_sig = jnp.minimum(z, 0.0) - jnp.log(den)
    lf = LOG2_E * (log_sig + jnp.log(1.0 + lb * jnp.exp(jnp.minimum(-z, HG_EXP_CLIP))))
    k = (1.0 - lb) * (jnp.where(z >= 0.0, e_abs, 1.0) / den)
    qs = q * _sigmoid(q)
    yield

    lane_a, blockdiag = _pair_masks()
    ti = lax.broadcasted_iota(jnp.int32, (CHUNK, PW), 0)
    si = lax.broadcasted_iota(jnp.int32, (CHUNK, PW), 1) & (CHUNK - 1)
    xr = ti ^ si
    lower = ti > si
    diag = ti == si
    picks = [lower & (xr >= (1 << l)) & (xr < (2 << l)) for l in range(N_LEVELS)]
    zero_b = jnp.zeros((CHUNK, PW), BF16)
    stack = lambda t: jnp.concatenate([jnp.where(lane_a, t, zero_b), jnp.where(lane_a, zero_b, t)], axis=0)
    k_b, v_b = k.astype(BF16), v.astype(BF16)

    e_mat = e_ref[...]
    inst = [(c, pr) for c in range(n_chunks) for pr in range(HG_PAIRS)]
    rows = lambda c: slice(c * CHUNK, (c + 1) * CHUNK)
    lanes = lambda pr: slice(pr * PW, (pr + 1) * PW)
    sums = [jnp.minimum(_dot_sel2(e_mat, lf[rows(c)]), 0.0) for c in range(n_chunks)]
    yield
    qb, kd, e_last, ql, kl = [], [], [], [], []
    for c in range(n_chunks):
        s_c, qs_c, k_c = sums[c], qs[rows(c)], k[rows(c)]
        b = s_c[0:CHUNK]
        b_last = b[CHUNK - 1:CHUNK]
        qb.append(qs_c * jnp.exp2(b))
        kd.append(k_c * jnp.exp2(b_last - b))
        e_last.append(jnp.exp2(b_last))
        ql_c, kl_c = [], []
        for l in range(N_LEVELS):
            w = jnp.exp2(s_c[(1 + l) * CHUNK:(2 + l) * CHUNK])
            ql_c.append((qs_c * w).astype(BF16))
            kl_c.append((k_c * w).astype(BF16))
        ql.append(ql_c)
        kl.append(kl_c)
        yield
    att = {}
    for c, pr in inst:
        att[c, pr] = jnp.where(diag, _dot(qs[rows(c), lanes(pr)], stack(k_b[rows(c), lanes(pr)]), NT), 0.0)
    yield
    for l in range(N_LEVELS):
        for c, pr in inst:
            att[c, pr] = jnp.where(picks[l], _dot(ql[c][l][:, lanes(pr)], stack(kl[c][l][:, lanes(pr)]), NT),
                                   att[c, pr])
        yield
    kv = {(c, pr): jnp.where(blockdiag, _dot(v_b[rows(c), lanes(pr)], kd[c][:, lanes(pr)], TN), 0.0)
          for c, pr in inst}
    yield
    o_intra = {(c, pr): _dot(att[c, pr], stack(v_b[rows(c), lanes(pr)])) for c, pr in inst}
    yield
    st_in = {}
    for pr in range(HG_PAIRS):
        st = st_ref[pr]
        for c in range(n_chunks):
            st_in[c, pr] = st
            st = st * e_last[c][:, lanes(pr)] + kv[c, pr]
        st_ref[pr] = st
    o = jnp.concatenate(
        [jnp.concatenate([_dot(qb[c][:, lanes(pr)], st_in[c, pr], NT) + o_intra[c, pr]
                          for pr in range(HG_PAIRS)], axis=-1) for c in range(n_chunks)], axis=0)
    yield
    ms = _dot(o * o, gm_ref[...]) * (1.0 / HD)
    o = o * lax.rsqrt(ms + EPS)
    y_ref[:, CONV_W:CONV_W + HG_W] = (o * ng_ref[...] * (gt * _sigmoid(gt))).astype(y_ref.dtype)
    yield


def _rwkv_stages(p_ref, mu_ref, w0_ref, w2_ref, a0_ref, a2_ref, g2_ref, kk_ref, ka_ref, rk_ref,
                 lnw_ref, lnb_ref, sel_ref, gm_ref, y_ref, st_ref, prev_ref, n_chunks):
    tb = n_chunks * CHUNK
    p = p_ref[...]
    rowp = lax.broadcasted_iota(jnp.int32, p.shape, 0)
    p_prev = jnp.where(rowp == 0, prev_ref[0:1, :], pltpu.roll(p, 1, 0))
    prev_ref[0:1, :] = p[tb - 1:tb, :]
    pf = p + (p_prev - p) * mu_ref[...]

    r = pf[:, :RW_W]
    k = pf[:, RW_W:2 * RW_W]
    v = pf[:, 2 * RW_W:3 * RW_W]
    o1 = 3 * RW_W
    wd = pf[:, o1:o1 + RW_DECAY_RANK]
    ad = pf[:, o1 + RW_DECAY_RANK:o1 + RW_DECAY_RANK + RW_A_RANK]
    gd = pf[:, o1 + RW_DECAY_RANK + RW_A_RANK:]

    lw = -(RW_DECAY_SCALE * LOG2_E) * _sigmoid(w0_ref[...] + _dot(jnp.tanh(wd), w2_ref[...]))
    a = _sigmoid(a0_ref[...] + _dot(ad, a2_ref[...]))
    g = _dot(_sigmoid(gd), g2_ref[...])
    yield

    gm = gm_ref[...]
    kk = k * kk_ref[...]
    kk = kk / jnp.maximum(jnp.sqrt(_dot(kk * kk, gm)), 1e-12)
    k = k * (1.0 + (a - 1.0) * ka_ref[...])
    a_vec = -kk
    b_vec = kk * a
    yield

    cw = _dot_sel2(sel_ref[...], lw)
    e_in = jnp.exp2(cw)
    e_out = jnp.exp2(-cw)
    e_rest = jnp.exp2(jnp.concatenate(
        [cw[(c + 1) * CHUNK - 1:(c + 1) * CHUNK] - cw[c * CHUNK:(c + 1) * CHUNK] for c in range(n_chunks)],
        axis=0))
    a_t = a_vec * jnp.exp2(cw - lw)
    r_t = r * e_in
    b_c = b_vec * e_out
    k_c = k * e_out
    b_h = b_vec * e_rest
    k_h = k * e_rest
    yield

    lane_a, blockdiag = _pair_masks()
    trow = lax.broadcasted_iota(jnp.int32, (CHUNK, PW), 0)
    scol = lax.broadcasted_iota(jnp.int32, (CHUNK, PW), 1) & (HD - 1)
    strict2 = trow > scol
    incl2 = trow >= scol
    eye_hi = jnp.where((trow == scol) & jnp.logical_not(lane_a), 1.0, 0.0)
    zero_blk = jnp.zeros((CHUNK, PW), BF16)
    under = lambda t: jnp.concatenate([jnp.zeros_like(t), t], axis=0)
    over = lambda t: jnp.concatenate([t, jnp.zeros_like(t)], axis=0)

    inst = [(c, pr) for c in range(n_chunks) for pr in range(RW_PAIRS)]
    heads = [(c, pr, hh) for c, pr in inst for hh in range(2)]
    blk = lambda t, c, pr: t[c * CHUNK:(c + 1) * CHUNK, pr * PW:(pr + 1) * PW]
    v_b = v.astype(BF16)
    abk = {}
    for c, pr in inst:
        at, rt = blk(a_t, c, pr), blk(r_t, c, pr)
        bc, kc = blk(b_c, c, pr), blk(k_c, c, pr)
        rhs = jnp.concatenate([jnp.where(lane_a, bc, 0.0), jnp.where(lane_a, kc, 0.0),
                               jnp.where(lane_a, 0.0, bc), jnp.where(lane_a, 0.0, kc)], axis=0)
        abk[c, pr] = _dot(jnp.concatenate([at, rt], axis=0), rhs, NT)
    yield
    cmat, rbk, av = {}, {}, {}
    for c, pr, hh in heads:
        hl = slice(hh * PW, (hh + 1) * PW)
        nk = jnp.where(strict2, abk[c, pr][:CHUNK, hl], 0.0)
        rbk[c, pr, hh] = jnp.where(incl2, abk[c, pr][CHUNK:, hl], 0.0)
        av[c, pr, hh] = _dot(nk, under(blk(v_b, c, pr)))
        cmat[c, pr, hh] = jnp.where(lane_a, nk, eye_hi)
    yield
    for _ in range(N_LEVELS):
        cb = {i: cmat[i].astype(BF16) for i in heads}
        cmat = {i: _dot(cb[i], over(cb[i])) + jnp.where(lane_a, 0.0, cmat[i]) for i in heads}
        yield
    zs = {i: _dot(cmat[i], under(jnp.concatenate([blk(a_t, i[0], i[1]), av[i]], axis=1))) for i in heads}
    yield
    rz = {i: _dot(rbk[i], jnp.concatenate(
        [zs[i].astype(BF16), jnp.concatenate([zero_blk, blk(v_b, i[0], i[1])], axis=1)], axis=0)) for i in heads}
    yield
    pick = lambda d, c, pr, sl: jnp.where(lane_a, d[c, pr, 0][:, sl], d[c, pr, 1][:, sl])
    lo, hi = slice(0, PW), slice(PW, 2 * PW)
    rbar, ybar, gmat, delta = {}, {}, {}, {}
    for c, pr in inst:
        bh, kh, vv = blk(b_h, c, pr).astype(BF16), blk(k_h, c, pr).astype(BF16), blk(v_b, c, pr)
        rbar[c, pr] = blk(r_t, c, pr) + pick(rz, c, pr, lo)
        ybar[c, pr] = pick(rz, c, pr, hi)
        gmat[c, pr] = jnp.where(blockdiag, _dot(pick(zs, c, pr, lo), bh, TN), 0.0)
        delta[c, pr] = jnp.where(blockdiag, _dot(jnp.concatenate([pick(zs, c, pr, hi).astype(BF16), vv], axis=0),
                                                 jnp.concatenate([bh, kh], axis=0), TN), 0.0)
    yield
    s = [st_ref[pr] for pr in range(RW_PAIRS)]
    ys = {}
    for c in range(n_chunks):
        for pr in range(RW_PAIRS):
            ys[c, pr] = _dot(rbar[c, pr], s[pr], NT) + ybar[c, pr]
        s = [s[pr] * blk(e_in, c, pr)[CHUNK - 1:CHUNK] + _dot(s[pr], gmat[c, pr]) + delta[c, pr]
             for pr in range(RW_PAIRS)]
        yield
    for pr in range(RW_PAIRS):
        st_ref[pr] = s[pr]
    y = jnp.concatenate([jnp.concatenate([ys[c, pr] for pr in range(RW_PAIRS)], axis=-1)
                         for c in range(n_chunks)], axis=0)

    inv_hd = 1.0 / HD
    mean = _dot(y, gm) * inv_hd
    yc = y - mean
    var = _dot(yc * yc, gm) * inv_hd
    yn = yc * lax.rsqrt(var + RW_GN_EPS) * lnw_ref[...] + lnb_ref[...]
    bonus = _dot(r * k * rk_ref[...], gm) * v
    y_ref[:, CONV_W + HG_W:] = ((yn + bonus) * g).astype(y_ref.dtype)
    yield


N_CONV_IN, N_HGRN_IN, N_RWKV_IN = 3, 4, 13


def _mixers_kernel(*refs, layer, n_chunks):
    it = iter(refs)
    take = lambda n: [next(it) for _ in range(n)]
    conv_in, hgrn_in, rwkv_in = take(N_CONV_IN), take(N_HGRN_IN), take(N_RWKV_IN)
    gm_ref, y_ref, conv_carry, hg_state, rw_state, rw_prev = take(6)

    @pl.when(pl.program_id(1) == 0)
    def _():
        for ref in (conv_carry, hg_state, rw_state, rw_prev):
            ref[...] = jnp.zeros_like(ref)

    stages = [
        _rwkv_stages(*rwkv_in, gm_ref, y_ref, rw_state, rw_prev, n_chunks),
        _hgrn_stages(*hgrn_in, gm_ref, y_ref, hg_state, layer, n_chunks),
        _conv_stages(*conv_in, y_ref, conv_carry),
    ]
    while stages:
        for gen in list(stages):
            if next(gen, stages) is stages:
                stages.remove(gen)


def _mixers(p_conv, p_hg, p_rw, conv_w, conv_b, lb_logits, hg_norm, mu, w0, w2, a0, a2, g2, k_k, k_a,
            r_k, ln_w, ln_b, layer, bsz, seqlen, n_chunks=MIX_BLOCK_CHUNKS):
    tb = n_chunks * CHUNK
    nb = seqlen // tb
    vec = lambda t: t.reshape(1, -1)
    row = lambda n: pl.BlockSpec((tb, n), lambda bi, j: (bi * nb + j, 0))
    conv_args = [p_conv, conv_w, vec(conv_b)]
    hgrn_args = [p_hg, lb_logits, vec(hg_norm), jnp.asarray(_hgrn_select_matrix(), BF16)]
    rwkv_args = [p_rw, vec(mu), vec(w0), w2.astype(BF16), vec(a0), a2.astype(BF16), g2.astype(BF16),
                 vec(k_k), vec(k_a), vec(r_k), vec(ln_w), vec(ln_b),
                 jnp.asarray(_block_select_matrix(n_chunks), BF16)]
    assert (len(conv_args), len(hgrn_args), len(rwkv_args)) == (N_CONV_IN, N_HGRN_IN, N_RWKV_IN)
    args = conv_args + hgrn_args + rwkv_args + [jnp.asarray(_head_group_matrix(HG_W), BF16)]
    streamed = {0: CONV_PROJ, N_CONV_IN: HG_PROJ, N_CONV_IN + N_HGRN_IN: RW_PROJ}
    in_specs = [row(streamed[i]) if i in streamed else _const_spec(t.shape) for i, t in enumerate(args)]
    mix_w = CONV_W + HG_W + RW_W
    return pl.pallas_call(
        functools.partial(_mixers_kernel, layer=layer, n_chunks=n_chunks),
        grid=(bsz, nb),
        in_specs=in_specs,
        out_specs=row(mix_w),
        out_shape=jax.ShapeDtypeStruct((bsz * seqlen, mix_w), BF16),
        scratch_shapes=[pltpu.VMEM((8, CONV_W), F32), pltpu.VMEM((HG_PAIRS, PW, PW), F32),
                        pltpu.VMEM((RW_PAIRS, PW, PW), F32), pltpu.VMEM((8, RW_PROJ), F32)],
        compiler_params=pltpu.CompilerParams(dimension_semantics=("arbitrary", "arbitrary"),
                                             vmem_limit_bytes=VMEM_LIMIT_BYTES),
        name="mixers",
    )(*args)


def _memkv_kernel(m_ref, g_ref, w_ref, k_ref, v_ref):
    mn = _rms(m_ref[...], g_ref[...]).astype(BF16)
    kv = jnp.dot(mn, w_ref[...], preferred_element_type=F32)
    k_ref[...] = kv[:, :D_MODEL].astype(BF16)
    v_ref[...] = kv[:, D_MODEL:].astype(BF16)


def _mem_kv(mem2d, g, wkv, layer, mem_len):
    t = mem2d.shape[0]
    row = pl.BlockSpec((mem_len, D_MODEL), lambda i: (i, 0))
    return pl.pallas_call(
        _memkv_kernel,
        grid=(t // mem_len,),
        in_specs=[row, _const_spec((1, D_MODEL)), _layer_spec((D_MODEL, 2 * D_MODEL), layer)],
        out_specs=[row, row],
        out_shape=[jax.ShapeDtypeStruct((t, D_MODEL), BF16)] * 2,
        compiler_params=pltpu.CompilerParams(dimension_semantics=("arbitrary",),
                                             vmem_limit_bytes=VMEM_LIMIT_BYTES),
        name="mem_kv",
    )(mem2d, g.reshape(1, D_MODEL), wkv)


def _xattn_kernel(h_ref, y_ref, wmo_ref, g_ref, wq_ref, mk_ref, mv_ref, wo_ref, o_ref):
    hm = h_ref.shape[0] // XATTN_SUBTILES
    halves = [slice(i * hm, (i + 1) * hm) for i in range(XATTN_SUBTILES)]
    heads = [slice(hd * XA_HEAD_DIM, (hd + 1) * XA_HEAD_DIM) for hd in range(XA_HEADS)]
    scale = XA_HEAD_DIM ** -0.5
    h = [h_ref[r, :] + jnp.dot(y_ref[r, :], wmo_ref[...], preferred_element_type=F32) for r in halves]
    q = [jnp.dot(_rms(t, g_ref[...]).astype(BF16), wq_ref[...], preferred_element_type=F32).astype(BF16)
         for t in h]
    scores = lambda i: [lax.dot_general(q[i][:, sl], mk_ref[:, sl], NT, preferred_element_type=F32) * scale
                        for sl in heads]

    def softmax(s_heads):
        out = []
        for s in s_heads:
            pe = jnp.exp(s - jnp.max(s, axis=-1, keepdims=True))
            out.append((pe / jnp.sum(pe, axis=-1, keepdims=True)).astype(BF16))
        return out

    attend = lambda pr: jnp.concatenate(
        [jnp.dot(p, mv_ref[:, sl], preferred_element_type=F32).astype(BF16) for p, sl in zip(pr, heads)], axis=-1)

    def project_out(i, o):
        o_ref[halves[i], :] = h[i] + jnp.dot(o, wo_ref[...], preferred_element_type=F32)

    s = [scores(0)]
    done = None
    for i in range(XATTN_SUBTILES):
        if i + 1 < XATTN_SUBTILES:
            s.append(scores(i + 1))
        p = softmax(s[i])
        if done is not None:
            project_out(*done)
        done = (i, attend(p))
    project_out(*done)


def _mixout_xattn(h, y, w_mix_out, g, wq, mk, mv, wo, layer, bsz, seqlen, mem_len, tq=512):
    nq = seqlen // tq
    row = lambda n: pl.BlockSpec((tq, n), lambda bi, j: (bi * nq + j, 0))
    memspec = pl.BlockSpec((mem_len, D_MODEL), lambda bi, j: (bi, 0))
    wspec = _layer_spec((D_MODEL, D_MODEL), layer)
    return pl.pallas_call(
        _xattn_kernel,
        grid=(bsz, nq),
        in_specs=[row(D_MODEL), row(D_MODEL),
                  wspec, _const_spec((1, D_MODEL)), wspec, memspec, memspec, wspec],
        out_specs=row(D_MODEL),
        out_shape=jax.ShapeDtypeStruct((bsz * seqlen, D_MODEL), F32),
        compiler_params=pltpu.CompilerParams(dimension_semantics=("arbitrary", "arbitrary"),
                                             vmem_limit_bytes=VMEM_LIMIT_BYTES),
        name="mixout_xattn",
    )(h, y, w_mix_out, g.reshape(1, D_MODEL), wq, mk, mv, wo)


def kernel(x, mem, ffn1_norm, ffn1_w_in, ffn1_w_out, mix_norm, w_mix_in, w_mix_out, conv_w, conv_b, hgrn_lb_logits, hgrn_norm, rwkv_mu, rwkv_w0, rwkv_w2, rwkv_a0, rwkv_a2, rwkv_g2, rwkv_k_k, rwkv_k_a, rwkv_r_k, rwkv_ln_w, rwkv_ln_b, xattn_norm, mem_norm, xattn_wq, xattn_wkv, xattn_wo, ffn2_norm, ffn2_w_in, ffn2_w_out, final_norm):
    bsz, seqlen, _ = x.shape
    mem_len = mem.shape[1]
    depth = ffn1_norm.shape[0]
    h = x.reshape(bsz * seqlen, D_MODEL)
    mem2d = mem.reshape(bsz * mem_len, D_MODEL)
    (ffn1_w_in, ffn1_w_out, ffn2_w_in, ffn2_w_out, w_mix_in, w_mix_out, xattn_wq, xattn_wkv,
     xattn_wo) = (w.astype(BF16) for w in (ffn1_w_in, ffn1_w_out, ffn2_w_in, ffn2_w_out, w_mix_in,
                                           w_mix_out, xattn_wq, xattn_wkv, xattn_wo))
    for l in range(depth):
        h = _ffn(h, ffn1_norm[l], ffn1_w_in, ffn1_w_out, l)
        p_conv, p_hg, p_rw = _mix_in(h, mix_norm[l], w_mix_in, l)
        y = _mixers(p_conv, p_hg, p_rw, conv_w[l], conv_b[l], hgrn_lb_logits, hgrn_norm[l],
                    rwkv_mu[l], rwkv_w0[l], rwkv_w2[l], rwkv_a0[l], rwkv_a2[l], rwkv_g2[l],
                    rwkv_k_k[l], rwkv_k_a[l], rwkv_r_k[l], rwkv_ln_w[l], rwkv_ln_b[l], l, bsz, seqlen)
        mk, mv = _mem_kv(mem2d, mem_norm[l], xattn_wkv, l, mem_len)
        h = _mixout_xattn(h, y, w_mix_out, xattn_norm[l], xattn_wq,
                          mk, mv, xattn_wo, l, bsz, seqlen, mem_len)
        h = _ffn(h, ffn2_norm[l], ffn2_w_in, ffn2_w_out, l,
                 final_g=final_norm if l == depth - 1 else None)
    return h.reshape(bsz, seqlen, D_MODEL)
```

```python
import functools

import numpy as np
import jax
import jax.numpy as jnp
from jax import lax
from jax.experimental import pallas as pl
from jax.experimental.pallas import tpu as pltpu

F32 = jnp.float32
BF16 = jnp.bfloat16

D_MODEL = 1024
DEPTH = 4
D_FF = 2816
FFN_RES = 0.5
EPS = 1e-6
CONV_W = 256
HD = 64
PW = 2 * HD
HG_W = 384
HG_PAIRS = HG_W // PW
HG_EXP_CLIP = 80.0
RW_W = 384
RW_PAIRS = RW_W // PW
RW_DECAY_RANK = 64
RW_A_RANK = 64
RW_G_RANK = 128
RW_PROJ = 3 * RW_W + RW_DECAY_RANK + RW_A_RANK + RW_G_RANK
RW_DECAY_SCALE = 0.606531
LOG2_E = 1.4426950408889634
RW_GN_EPS = 64e-5
CONV_PROJ = 3 * CONV_W
HG_PROJ = 4 * HG_W
IN_W = CONV_PROJ + HG_PROJ + RW_PROJ
CHUNK = 64
N_LEVELS = 6
MIX_BLOCK_CHUNKS = 4
XA_HEADS = 4
XA_HEAD_DIM = D_MODEL // XA_HEADS
XATTN_SUBTILES = 2

VMEM_LIMIT_BYTES = 56 * 1024 * 1024
MXU_TILE = 256
FFN_SPLITS = (0, 6 * MXU_TILE, D_FF)
assert all(b % MXU_TILE == 0 for b in FFN_SPLITS)

NN = (((1,), (0,)), ((), ()))
NT = (((1,), (1,)), ((), ()))
TN = (((0,), (0,)), ((), ()))


def _dot(a, b, dims=NN):
    return lax.dot_general(a.astype(BF16), b.astype(BF16), dims, preferred_element_type=F32)


def _dot_sel2(e, x):
    hi = x.astype(BF16)
    lo = (x - hi.astype(F32)).astype(BF16)
    d = lambda t: lax.dot_general(e, t, NN, preferred_element_type=F32)
    return d(hi) + d(lo)


def _rms(x, g):
    return x * lax.rsqrt(jnp.mean(x * x, axis=-1, keepdims=True) + EPS) * g


def _sigmoid(x):
    return jax.nn.sigmoid(x)


def _ffn_kernel(x_ref, g_ref, win_ref, wout_ref, *rest, final):
    if final:
        fg_ref, o_ref = rest
    else:
        (o_ref,) = rest
    x = x_ref[...]
    xn = _rms(x, g_ref[...]).astype(BF16)
    acc = jnp.zeros_like(x)
    for lo, hi in zip(FFN_SPLITS[:-1], FFN_SPLITS[1:]):
        gate = jnp.dot(xn, win_ref[:, lo:hi], preferred_element_type=F32)
        up = jnp.dot(xn, win_ref[:, D_FF + lo:D_FF + hi], preferred_element_type=F32)
        act = (gate * _sigmoid(gate) * up).astype(BF16)
        acc = acc + jnp.dot(act, wout_ref[lo:hi, :], preferred_element_type=F32)
    out = x + FFN_RES * acc
    if final:
        out = _rms(out, fg_ref[...])
    o_ref[...] = out


def _const_spec(shape):
    nd = len(shape)
    return pl.BlockSpec(shape, lambda *_: (0,) * nd)


def _layer_spec(shape, layer):
    return pl.BlockSpec((None,) + tuple(shape), lambda *_: (layer, 0, 0))


def _ffn(h, g, w_in, w_out, layer, final_g=None, tm=512):
    t = h.shape[0]
    final = final_g is not None
    in_specs = [pl.BlockSpec((tm, D_MODEL), lambda i: (i, 0)),
                _const_spec((1, D_MODEL)),
                _layer_spec((D_MODEL, 2 * D_FF), layer),
                _layer_spec((D_FF, D_MODEL), layer)]
    args = [h, g.reshape(1, D_MODEL), w_in, w_out]
    if final:
        in_specs.append(_const_spec((1, D_MODEL)))
        args.append(final_g.reshape(1, D_MODEL))
    return pl.pallas_call(
        functools.partial(_ffn_kernel, final=final),
        grid=(t // tm,),
        in_specs=in_specs,
        out_specs=pl.BlockSpec((tm, D_MODEL), lambda i: (i, 0)),
        out_shape=jax.ShapeDtypeStruct((t, D_MODEL), F32),
        compiler_params=pltpu.CompilerParams(dimension_semantics=("arbitrary",),
                                             vmem_limit_bytes=VMEM_LIMIT_BYTES),
        name="ffn_final" if final else "ffn",
    )(*args)


def _mixin_kernel(x_ref, g_ref, w_ref, pc_ref, ph_ref, pr_ref):
    xn = _rms(x_ref[...], g_ref[...]).astype(BF16)
    p = jnp.dot(xn, w_ref[...], preferred_element_type=F32)
    pc_ref[...] = p[:, :CONV_PROJ]
    ph_ref[...] = p[:, CONV_PROJ:CONV_PROJ + HG_PROJ]
    pr_ref[...] = p[:, CONV_PROJ + HG_PROJ:]


def _mix_in(h, g, w, layer, tm=512):
    t = h.shape[0]
    row = lambda n: pl.BlockSpec((tm, n), lambda i: (i, 0))
    return pl.pallas_call(
        _mixin_kernel,
        grid=(t // tm,),
        in_specs=[row(D_MODEL), _const_spec((1, D_MODEL)), _layer_spec((D_MODEL, IN_W), layer)],
        out_specs=[row(CONV_PROJ), row(HG_PROJ), row(RW_PROJ)],
        out_shape=[jax.ShapeDtypeStruct((t, CONV_PROJ), F32),
                   jax.ShapeDtypeStruct((t, HG_PROJ), F32),
                   jax.ShapeDtypeStruct((t, RW_PROJ), F32)],
        compiler_params=pltpu.CompilerParams(dimension_semantics=("arbitrary",),
                                             vmem_limit_bytes=VMEM_LIMIT_BYTES),
        name="mix_in",
    )(h, g.reshape(1, D_MODEL), w)


def _conv_stages(p_ref, w_ref, b_ref, y_ref, carry_ref):
    halves = 2
    tb = p_ref.shape[0]
    hb = tb // halves
    for i in range(halves):
        p = p_ref[i * hb:(i + 1) * hb, :]
        b_gate = p[:, :CONV_W]
        z = p[:, CONV_W:2 * CONV_W] * p[:, 2 * CONV_W:]
        row = lax.broadcasted_iota(jnp.int32, z.shape, 0)
        c0 = carry_ref[0:1, :]
        c1 = carry_ref[1:2, :]
        z1 = jnp.where(row == 0, c1, pltpu.roll(z, 1, 0))
        z2 = jnp.where(row == 0, c0, jnp.where(row == 1, c1, pltpu.roll(z, 2, 0)))
        w = w_ref[...]
        zc = w[0:1, :] * z2 + w[1:2, :] * z1 + w[2:3, :] * z
        y_ref[i * hb:(i + 1) * hb, 0:CONV_W] = (b_gate * (zc + b_ref[...])).astype(y_ref.dtype)
        carry_ref[0:2, :] = z[hb - 2:hb, :]
        yield


def _hgrn_select_matrix():
    c = CHUNK
    e = np.zeros((1 + N_LEVELS, c, c), np.float32)
    for i in range(c):
        e[0, i, :i + 1] = 1.0
        for l in range(N_LEVELS):
            half = 1 << l
            m = (i // (2 * half)) * 2 * half + half - 1
            if i > m:
                e[1 + l, i, m + 1:i + 1] = 1.0
            else:
                e[1 + l, i, i + 1:m + 1] = 1.0
    return e.reshape((1 + N_LEVELS) * c, c)


def _head_group_matrix(width):
    idx = np.arange(width) // HD
    return (idx[:, None] == idx[None, :]).astype(np.float32)


def _block_select_matrix(n_chunks):
    i = np.arange(n_chunks * CHUNK)
    same = (i[:, None] // CHUNK) == (i[None, :] // CHUNK)
    return (same & (i[None, :] <= i[:, None])).astype(np.float32)


def _pair_masks():
    lane_a = lax.broadcasted_iota(jnp.int32, (CHUNK, PW), 1) < HD
    bi = lax.broadcasted_iota(jnp.int32, (PW, PW), 0) < HD
    bj = lax.broadcasted_iota(jnp.int32, (PW, PW), 1) < HD
    return lane_a, bi == bj


def _hgrn_stages(p_ref, lbl_ref, ng_ref, e_ref, gm_ref, y_ref, st_ref, layer, n_chunks):
    p = p_ref[...]
    q = p[:, :HG_W]
    z = p[:, HG_W:2 * HG_W]
    v = p[:, 2 * HG_W:3 * HG_W]
    gt = p[:, 3 * HG_W:]

    lg = lbl_ref[...]
    ex = jnp.exp(lg - jnp.max(lg, axis=0, keepdims=True))
    sm = ex / jnp.sum(ex, axis=0, keepdims=True)
    lb = jnp.maximum(jnp.sum(sm[:layer + 1], axis=0, keepdims=True) - sm[0:1], 0.0)

    e_abs = jnp.exp(-jnp.abs(z))
    den = 1.0 + e_abs
    log_sig = jnp.minimum(z, 0.0) - jnp.log(den)
    lf = LOG2_E * (log_sig + jnp.log(1.0 + lb * jnp.exp(jnp.minimum(-z, HG_EXP_CLIP))))
    k = (1.0 - lb) * (jnp.where(z >= 0.0, e_abs, 1.0) / den)
    qs = q * _sigmoid(q)
    yield

    lane_a, blockdiag = _pair_masks()
    ti = lax.broadcasted_iota(jnp.int32, (CHUNK, PW), 0)
    si = lax.broadcasted_iota(jnp.int32, (CHUNK, PW), 1) & (CHUNK - 1)
    xr = ti ^ si
    lower = ti > si
    diag = ti == si
    picks = [lower & (xr >= (1 << l)) & (xr < (2 << l)) for l in range(N_LEVELS)]
    zero_b = jnp.zeros((CHUNK, PW), BF16)
    stack = lambda t: jnp.concatenate([jnp.where(lane_a, t, zero_b), jnp.where(lane_a, zero_b, t)], axis=0)
    qs_b, k_b, v_b = qs.astype(BF16), k.astype(BF16), v.astype(BF16)

    e_mat = e_ref[...]
    inst = [(c, pr) for c in range(n_chunks) for pr in range(HG_PAIRS)]
    rows = lambda c: slice(c * CHUNK, (c + 1) * CHUNK)
    lanes = lambda pr: slice(pr * PW, (pr + 1) * PW)
    sums = [jnp.minimum(_dot_sel2(e_mat, lf[rows(c)]), 0.0) for c in range(n_chunks)]
    yield
    qb, kd, e_last, ql, kl = [], [], [], [], []
    for c in range(n_chunks):
        s_c, qs_cb, k_cb = sums[c], qs_b[rows(c)], k_b[rows(c)]
        b = s_c[0:CHUNK]
        b_last = b[CHUNK - 1:CHUNK]
        qb.append(qs_cb * jnp.exp2(b).astype(BF16))
        kd.append(k_cb * jnp.exp2(b_last - b).astype(BF16))
        e_last.append(jnp.exp2(b_last))
        ql_c, kl_c = [], []
        for l in range(N_LEVELS):
            w = jnp.exp2(s_c[(1 + l) * CHUNK:(2 + l) * CHUNK]).astype(BF16)
            ql_c.append(qs_cb * w)
            kl_c.append(k_cb * w)
        ql.append(ql_c)
        kl.append(kl_c)
        yield
    att = {}
    for c, pr in inst:
        att[c, pr] = jnp.where(diag, _dot(qs_b[rows(c), lanes(pr)], stack(k_b[rows(c), lanes(pr)]), NT), 0.0)
    yield
    for l in range(N_LEVELS):
        for c, pr in inst:
            att[c, pr] = jnp.where(picks[l], _dot(ql[c][l][:, lanes(pr)], stack(kl[c][l][:, lanes(pr)]), NT),
                                   att[c, pr])
        yield
    kv = {(c, pr): jnp.where(blockdiag, _dot(v_b[rows(c), lanes(pr)], kd[c][:, lanes(pr)], TN), 0.0)
          for c, pr in inst}
    yield
    o_intra = {(c, pr): _dot(att[c, pr], stack(v_b[rows(c), lanes(pr)])) for c, pr in inst}
    yield
    st_in = {}
    for pr in range(HG_PAIRS):
        st = st_ref[pr]
        for c in range(n_chunks):
            st_in[c, pr] = st
            st = st * e_last[c][:, lanes(pr)] + kv[c, pr]
        st_ref[pr] = st
    o = jnp.concatenate(
        [jnp.concatenate([_dot(qb[c][:, lanes(pr)], st_in[c, pr], NT) + o_intra[c, pr]
                          for pr in range(HG_PAIRS)], axis=-1) for c in range(n_chunks)], axis=0)
    yield
    ms = _dot(o * o, gm_ref[...]) * (1.0 / HD)
    o = o * lax.rsqrt(ms + EPS)
    y_ref[:, CONV_W:CONV_W + HG_W] = (o * ng_ref[...] * (gt * _sigmoid(gt))).astype(y_ref.dtype)
    yield


def _rwkv_stages(p_ref, mu_ref, w0_ref, w2_ref, a0_ref, a2_ref, g2_ref, kk_ref, ka_ref, rk_ref,
                 lnw_ref, lnb_ref, sel_ref, gm_ref, y_ref, st_ref, prev_ref, n_chunks):
    tb = n_chunks * CHUNK
    p = p_ref[...]
    rowp = lax.broadcasted_iota(jnp.int32, p.shape, 0)
    p_prev = jnp.where(rowp == 0, prev_ref[0:1, :], pltpu.roll(p, 1, 0))
    prev_ref[0:1, :] = p[tb - 1:tb, :]
    pf = p + (p_prev - p) * mu_ref[...]

    r = pf[:, :RW_W]
    k = pf[:, RW_W:2 * RW_W]
    v = pf[:, 2 * RW_W:3 * RW_W]
    o1 = 3 * RW_W
    wd = pf[:, o1:o1 + RW_DECAY_RANK]
    ad = pf[:, o1 + RW_DECAY_RANK:o1 + RW_DECAY_RANK + RW_A_RANK]
    gd = pf[:, o1 + RW_DECAY_RANK + RW_A_RANK:]

    lw = -(RW_DECAY_SCALE * LOG2_E) * _sigmoid(w0_ref[...] + _dot(jnp.tanh(wd), w2_ref[...]))
    a = _sigmoid(a0_ref[...] + _dot(ad, a2_ref[...]))
    g = _dot(_sigmoid(gd), g2_ref[...])
    yield

    gm = gm_ref[...]
    kk = k * kk_ref[...]
    kk = kk / jnp.maximum(jnp.sqrt(_dot(kk * kk, gm)), 1e-12)
    k = k * (1.0 + (a - 1.0) * ka_ref[...])
    a_vec = -kk
    b_vec = kk * a
    yield

    cw = _dot_sel2(sel_ref[...], lw)
    e_in = jnp.exp2(cw)
    e_out = jnp.exp2(-cw)
    e_rest = jnp.exp2(jnp.concatenate(
        [cw[(c + 1) * CHUNK - 1:(c + 1) * CHUNK] - cw[c * CHUNK:(c + 1) * CHUNK] for c in range(n_chunks)],
        axis=0))
    a_t = a_vec * jnp.exp2(cw - lw)
    r_t = r * e_in
    bv_b, k_b, eo_b, er_b = (t.astype(BF16) for t in (b_vec, k, e_out, e_rest))
    b_c = bv_b * eo_b
    k_c = k_b * eo_b
    b_h = bv_b * er_b
    k_h = k_b * er_b
    yield

    lane_a, blockdiag = _pair_masks()
    trow = lax.broadcasted_iota(jnp.int32, (CHUNK, PW), 0)
    scol = lax.broadcasted_iota(jnp.int32, (CHUNK, PW), 1) & (HD - 1)
    strict2 = trow > scol
    incl2 = trow >= scol
    eye_hi = jnp.where((trow == scol) & jnp.logical_not(lane_a), 1.0, 0.0)
    zero_blk = jnp.zeros((CHUNK, PW), BF16)
    under = lambda t: jnp.concatenate([jnp.zeros_like(t), t], axis=0)
    over = lambda t: jnp.concatenate([t, jnp.zeros_like(t)], axis=0)

    inst = [(c, pr) for c in range(n_chunks) for pr in range(RW_PAIRS)]
    heads = [(c, pr, hh) for c, pr in inst for hh in range(2)]
    blk = lambda t, c, pr: t[c * CHUNK:(c + 1) * CHUNK, pr * PW:(pr + 1) * PW]
    v_b = v.astype(BF16)
    abk = {}
    for c, pr in inst:
        at, rt = blk(a_t, c, pr), blk(r_t, c, pr)
        bc, kc = blk(b_c, c, pr), blk(k_c, c, pr)
        rhs = jnp.concatenate([jnp.where(lane_a, bc, zero_blk), jnp.where(lane_a, kc, zero_blk),
                               jnp.where(lane_a, zero_blk, bc), jnp.where(lane_a, zero_blk, kc)], axis=0)
        abk[c, pr] = _dot(jnp.concatenate([at, rt], axis=0), rhs, NT)
    yield
    cmat, rbk, av = {}, {}, {}
    for c, pr, hh in heads:
        hl = slice(hh * PW, (hh + 1) * PW)
        nk = jnp.where(strict2, abk[c, pr][:CHUNK, hl], 0.0)
        rbk[c, pr, hh] = jnp.where(incl2, abk[c, pr][CHUNK:, hl], 0.0)
        av[c, pr, hh] = _dot(nk, under(blk(v_b, c, pr)))
        cmat[c, pr, hh] = jnp.where(lane_a, nk, eye_hi)
    yield
    for _ in range(N_LEVELS):
        cb = {i: cmat[i].astype(BF16) for i in heads}
        cmat = {i: _dot(cb[i], over(cb[i])) + jnp.where(lane_a, 0.0, cmat[i]) for i in heads}
        yield
    zs = {i: _dot(cmat[i], under(jnp.concatenate([blk(a_t, i[0], i[1]), av[i]], axis=1))) for i in heads}
    yield
    rz = {i: _dot(rbk[i], jnp.concatenate(
        [zs[i].astype(BF16), jnp.concatenate([zero_blk, blk(v_b, i[0], i[1])], axis=1)], axis=0)) for i in heads}
    yield
    pick = lambda d, c, pr, sl: jnp.where(lane_a, d[c, pr, 0][:, sl], d[c, pr, 1][:, sl])
    lo, hi = slice(0, PW), slice(PW, 2 * PW)
    rbar, ybar, gmat, delta = {}, {}, {}, {}
    for c, pr in inst:
        bh, kh, vv = blk(b_h, c, pr), blk(k_h, c, pr), blk(v_b, c, pr)
        rbar[c, pr] = blk(r_t, c, pr) + pick(rz, c, pr, lo)
        ybar[c, pr] = pick(rz, c, pr, hi)
        gmat[c, pr] = jnp.where(blockdiag, _dot(pick(zs, c, pr, lo), bh, TN), 0.0)
        delta[c, pr] = jnp.where(blockdiag, _dot(jnp.concatenate([pick(zs, c, pr, hi).astype(BF16), vv], axis=0),
                                                 jnp.concatenate([bh, kh], axis=0), TN), 0.0)
    yield
    s = [st_ref[pr] for pr in range(RW_PAIRS)]
    ys = {}
    for c in range(n_chunks):
        for pr in range(RW_PAIRS):
            ys[c, pr] = _dot(rbar[c, pr], s[pr], NT) + ybar[c, pr]
        s = [s[pr] * blk(e_in, c, pr)[CHUNK - 1:CHUNK] + _dot(s[pr], gmat[c, pr]) + delta[c, pr]
             for pr in range(RW_PAIRS)]
        yield
    for pr in range(RW_PAIRS):
        st_ref[pr] = s[pr]
    y = jnp.concatenate([jnp.concatenate([ys[c, pr] for pr in range(RW_PAIRS)], axis=-1)
                         for c in range(n_chunks)], axis=0)

    inv_hd = 1.0 / HD
    mean = _dot(y, gm) * inv_hd
    yc = y - mean
    var = _dot(yc * yc, gm) * inv_hd
    yn = yc * lax.rsqrt(var + RW_GN_EPS) * lnw_ref[...] + lnb_ref[...]
    bonus = _dot(r * k * rk_ref[...], gm) * v
    y_ref[:, CONV_W + HG_W:] = ((yn + bonus) * g).astype(y_ref.dtype)
    yield


N_CONV_IN, N_HGRN_IN, N_RWKV_IN = 3, 4, 13


def _mixers_kernel(*refs, layer, n_chunks):
    it = iter(refs)
    take = lambda n: [next(it) for _ in range(n)]
    conv_in, hgrn_in, rwkv_in = take(N_CONV_IN), take(N_HGRN_IN), take(N_RWKV_IN)
    gm_ref, y_ref, conv_carry, hg_state, rw_state, rw_prev = take(6)

    @pl.when(pl.program_id(1) == 0)
    def _():
        for ref in (conv_carry, hg_state, rw_state, rw_prev):
            ref[...] = jnp.zeros_like(ref)

    stages = [
        _rwkv_stages(*rwkv_in, gm_ref, y_ref, rw_state, rw_prev, n_chunks),
        _hgrn_stages(*hgrn_in, gm_ref, y_ref, hg_state, layer, n_chunks),
        _conv_stages(*conv_in, y_ref, conv_carry),
    ]
    while stages:
        for gen in list(stages):
            if next(gen, stages) is stages:
                stages.remove(gen)


def _mixers(p_conv, p_hg, p_rw, conv_w, conv_b, lb_logits, hg_norm, mu, w0, w2, a0, a2, g2, k_k, k_a,
            r_k, ln_w, ln_b, layer, bsz, seqlen, n_chunks=MIX_BLOCK_CHUNKS):
    tb = n_chunks * CHUNK
    nb = seqlen // tb
    vec = lambda t: t.reshape(1, -1)
    row = lambda n: pl.BlockSpec((tb, n), lambda bi, j: (bi * nb + j, 0))
    conv_args = [p_conv, conv_w, vec(conv_b)]
    hgrn_args = [p_hg, lb_logits, vec(hg_norm), jnp.asarray(_hgrn_select_matrix(), BF16)]
    rwkv_args = [p_rw, vec(mu), vec(w0), w2.astype(BF16), vec(a0), a2.astype(BF16), g2.astype(BF16),
                 vec(k_k), vec(k_a), vec(r_k), vec(ln_w), vec(ln_b),
                 jnp.asarray(_block_select_matrix(n_chunks), BF16)]
    assert (len(conv_args), len(hgrn_args), len(rwkv_args)) == (N_CONV_IN, N_HGRN_IN, N_RWKV_IN)
    args = conv_args + hgrn_args + rwkv_args + [jnp.asarray(_head_group_matrix(HG_W), BF16)]
    streamed = {0: CONV_PROJ, N_CONV_IN: HG_PROJ, N_CONV_IN + N_HGRN_IN: RW_PROJ}
    in_specs = [row(streamed[i]) if i in streamed else _const_spec(t.shape) for i, t in enumerate(args)]
    mix_w = CONV_W + HG_W + RW_W
    return pl.pallas_call(
        functools.partial(_mixers_kernel, layer=layer, n_chunks=n_chunks),
        grid=(bsz, nb),
        in_specs=in_specs,
        out_specs=row(mix_w),
        out_shape=jax.ShapeDtypeStruct((bsz * seqlen, mix_w), BF16),
        scratch_shapes=[pltpu.VMEM((8, CONV_W), F32), pltpu.VMEM((HG_PAIRS, PW, PW), F32),
                        pltpu.VMEM((RW_PAIRS, PW, PW), F32), pltpu.VMEM((8, RW_PROJ), F32)],
        compiler_params=pltpu.CompilerParams(dimension_semantics=("arbitrary", "arbitrary"),
                                             vmem_limit_bytes=VMEM_LIMIT_BYTES),
        name="mixers",
    )(*args)


def _memkv_kernel(m_ref, g_ref, w_ref, k_ref, v_ref):
    mn = _rms(m_ref[...], g_ref[...]).astype(BF16)
    kv = jnp.dot(mn, w_ref[...], preferred_element_type=F32)
    k_ref[...] = kv[:, :D_MODEL].astype(BF16)
    v_ref[...] = kv[:, D_MODEL:].astype(BF16)


def _mem_kv(mem2d, g, wkv, layer, mem_len):
    t = mem2d.shape[0]
    row = pl.BlockSpec((mem_len, D_MODEL), lambda i: (i, 0))
    return pl.pallas_call(
        _memkv_kernel,
        grid=(t // mem_len,),
        in_specs=[row, _const_spec((1, D_MODEL)), _layer_spec((D_MODEL, 2 * D_MODEL), layer)],
        out_specs=[row, row],
        out_shape=[jax.ShapeDtypeStruct((t, D_MODEL), BF16)] * 2,
        compiler_params=pltpu.CompilerParams(dimension_semantics=("arbitrary",),
                                             vmem_limit_bytes=VMEM_LIMIT_BYTES),
        name="mem_kv",
    )(mem2d, g.reshape(1, D_MODEL), wkv)


def _xattn_kernel(h_ref, y_ref, wmo_ref, g_ref, wq_ref, mk_ref, mv_ref, wo_ref, o_ref):
    hm = h_ref.shape[0] // XATTN_SUBTILES
    halves = [slice(i * hm, (i + 1) * hm) for i in range(XATTN_SUBTILES)]
    heads = [slice(hd * XA_HEAD_DIM, (hd + 1) * XA_HEAD_DIM) for hd in range(XA_HEADS)]
    scale = XA_HEAD_DIM ** -0.5
    h = [h_ref[r, :] + jnp.dot(y_ref[r, :], wmo_ref[...], preferred_element_type=F32) for r in halves]
    q = [jnp.dot(_rms(t, g_ref[...]).astype(BF16), wq_ref[...], preferred_element_type=F32).astype(BF16)
         for t in h]
    scores = lambda i: [lax.dot_general(q[i][:, sl], mk_ref[:, sl], NT, preferred_element_type=F32) * scale
                        for sl in heads]

    def softmax(s_heads):
        out = []
        for s in s_heads:
            pe = jnp.exp(s - jnp.max(s, axis=-1, keepdims=True))
            out.append((pe / jnp.sum(pe, axis=-1, keepdims=True)).astype(BF16))
        return out

    attend = lambda pr: jnp.concatenate(
        [jnp.dot(p, mv_ref[:, sl], preferred_element_type=F32).astype(BF16) for p, sl in zip(pr, heads)], axis=-1)

    def project_out(i, o):
        o_ref[halves[i], :] = h[i] + jnp.dot(o, wo_ref[...], preferred_element_type=F32)

    s = [scores(0)]
    done = None
    for i in range(XATTN_SUBTILES):
        if i + 1 < XATTN_SUBTILES:
            s.append(scores(i + 1))
        p = softmax(s[i])
        if done is not None:
            project_out(*done)
        done = (i, attend(p))
    project_out(*done)


def _mixout_xattn(h, y, w_mix_out, g, wq, mk, mv, wo, layer, bsz, seqlen, mem_len, tq=512):
    nq = seqlen // tq
    row = lambda n: pl.BlockSpec((tq, n), lambda bi, j: (bi * nq + j, 0))
    memspec = pl.BlockSpec((mem_len, D_MODEL), lambda bi, j: (bi, 0))
    wspec = _layer_spec((D_MODEL, D_MODEL), layer)
    return pl.pallas_call(
        _xattn_kernel,
        grid=(bsz, nq),
        in_specs=[row(D_MODEL), row(D_MODEL),
                  wspec, _const_spec((1, D_MODEL)), wspec, memspec, memspec, wspec],
        out_specs=row(D_MODEL),
        out_shape=jax.ShapeDtypeStruct((bsz * seqlen, D_MODEL), F32),
        compiler_params=pltpu.CompilerParams(dimension_semantics=("arbitrary", "arbitrary"),
                                             vmem_limit_bytes=VMEM_LIMIT_BYTES),
        name="mixout_xattn",
    )(h, y, w_mix_out, g.reshape(1, D_MODEL), wq, mk, mv, wo)


def kernel(x, mem, ffn1_norm, ffn1_w_in, ffn1_w_out, mix_norm, w_mix_in, w_mix_out, conv_w, conv_b, hgrn_lb_logits, hgrn_norm, rwkv_mu, rwkv_w0, rwkv_w2, rwkv_a0, rwkv_a2, rwkv_g2, rwkv_k_k, rwkv_k_a, rwkv_r_k, rwkv_ln_w, rwkv_ln_b, xattn_norm, mem_norm, xattn_wq, xattn_wkv, xattn_wo, ffn2_norm, ffn2_w_in, ffn2_w_out, final_norm):
    bsz, seqlen, _ = x.shape
    mem_len = mem.shape[1]
    depth = ffn1_norm.shape[0]
    h = x.reshape(bsz * seqlen, D_MODEL)
    mem2d = mem.reshape(bsz * mem_len, D_MODEL)
    (ffn1_w_in, ffn1_w_out, ffn2_w_in, ffn2_w_out, w_mix_in, w_mix_out, xattn_wq, xattn_wkv,
     xattn_wo) = (w.astype(BF16) for w in (ffn1_w_in, ffn1_w_out, ffn2_w_in, ffn2_w_out, w_mix_in,
                                           w_mix_out, xattn_wq, xattn_wkv, xattn_wo))
    for l in range(depth):
        h = _ffn(h, ffn1_norm[l], ffn1_w_in, ffn1_w_out, l)
        p_conv, p_hg, p_rw = _mix_in(h, mix_norm[l], w_mix_in, l)
        y = _mixers(p_conv, p_hg, p_rw, conv_w[l], conv_b[l], hgrn_lb_logits, hgrn_norm[l],
                    rwkv_mu[l], rwkv_w0[l], rwkv_w2[l], rwkv_a0[l], rwkv_a2[l], rwkv_g2[l],
                    rwkv_k_k[l], rwkv_k_a[l], rwkv_r_k[l], rwkv_ln_w[l], rwkv_ln_b[l], l, bsz, seqlen)
        mk, mv = _mem_kv(mem2d, mem_norm[l], xattn_wkv, l, mem_len)
        h = _mixout_xattn(h, y, w_mix_out, xattn_norm[l], xattn_wq,
                          mk, mv, xattn_wo, l, bsz, seqlen, mem_len)
        h = _ffn(h, ffn2_norm[l], ffn2_w_in, ffn2_w_out, l,
                 final_g=final_norm if l == depth - 1 else None)
    return h.reshape(bsz, seqlen, D_MODEL)
```
